```python
import math
import jax, jax.numpy as jnp
from jax import lax
import numpy as np

D_MODEL = 1024
BATCH = 8
SEQ = 2048
DEPTH = 2

HEAD_DIM = 64
A_HEADS = 4
DILATED_PATTERNS = ((128, 1), (512, 4), (2048, 16))
WIN_BLOCK = 64
B_HEADS = 8
MLA_Q_LORA = 384
MLA_KV_LORA = 256
MLA_NOPE = 64
MLA_ROPE = 32
MLA_V = 64
ROPE_THETA = 10000.0
C_HEADS = 4
DIFF_QK = 32
DIFF_V = 64
D_FF = 2816
QUERY_BLOCK = 128
EPS = 1e-6
NEG_INF = -1e30

A_COLS = 3 * A_HEADS * HEAD_DIM
B_COLS = MLA_Q_LORA + MLA_KV_LORA + MLA_ROPE
C_COLS = 2 * (C_HEADS * 2 * DIFF_QK) + C_HEADS * DIFF_V
N_IN = A_COLS + B_COLS + C_COLS
D_MIX = A_HEADS * HEAD_DIM + B_HEADS * MLA_V + C_HEADS * DIFF_V

kernel_name = "hybrid_dilated_mla_diff_macaron_encoder"


def rmsnorm(x, gain):
    xf = x.astype(jnp.float32)
    y = xf * lax.rsqrt(jnp.mean(xf * xf, axis=-1, keepdims=True) + EPS)
    return y.astype(x.dtype) * gain


def swiglu(h, w_gate, w_up, w_down):
    return (jax.nn.silu(h @ w_gate) * (h @ w_up)) @ w_down


def alibi_slopes():
    n = A_HEADS + C_HEADS
    s = jnp.asarray([2.0 ** (-8.0 * (i + 1) / n) for i in range(n)], dtype=jnp.float32)
    return s[0::2], s[1::2]


def rope(t, pos):
    half = t.shape[-1] // 2
    inv = ROPE_THETA ** (-jnp.arange(half, dtype=jnp.float32) / half)
    ang = pos.astype(jnp.float32)[..., None, None] * inv
    cos, sin = jnp.cos(ang), jnp.sin(ang)
    tf = t.astype(jnp.float32)
    t1, t2 = tf[..., :half], tf[..., half:]
    return jnp.concatenate([t1 * cos - t2 * sin, t2 * cos + t1 * sin], axis=-1).astype(t.dtype)


def dilated_branch(q, k, v, pos, slopes, dilation, radius):
    B, S, H, E = q.shape
    L = S // dilation

    def to_res(t):
        return t.reshape(B, L, dilation, H, E).transpose(0, 2, 3, 1, 4)

    qr, kr, vr = to_res(q), to_res(k), to_res(v)
    pr = pos.reshape(B, L, dilation).transpose(0, 2, 1)
    nb = -(-L // WIN_BLOCK)
    Lp = nb * WIN_BLOCK
    K = WIN_BLOCK + 2 * radius
    qr = jnp.pad(qr, ((0, 0), (0, 0), (0, 0), (0, Lp - L), (0, 0)))
    kpad = ((0, 0), (0, 0), (0, 0), (radius, radius + Lp - L), (0, 0))
    kr, vr = jnp.pad(kr, kpad), jnp.pad(vr, kpad)
    pq = jnp.pad(pr, ((0, 0), (0, 0), (0, Lp - L)))
    pk = jnp.pad(pr, ((0, 0), (0, 0), (radius, radius + Lp - L)))
    slab = jnp.arange(nb)[:, None] * WIN_BLOCK + jnp.arange(K)[None, :]
    ks = jnp.take(kr, slab, axis=3)
    vs = jnp.take(vr, slab, axis=3)
    pks = jnp.take(pk, slab, axis=2)
    qb = qr.reshape(B, dilation, H, nb, WIN_BLOCK, E)
    pqb = pq.reshape(B, dilation, nb, WIN_BLOCK)
    off = jnp.arange(K)[None, :] - radius - jnp.arange(WIN_BLOCK)[:, None]
    kabs = slab - radius
    valid = (jnp.abs(off) <= radius)[None] & ((kabs >= 0) & (kabs < L))[:, None, :]
    sc = jnp.einsum('bdhnqe,bdhnke->bdhnqk', qb, ks).astype(jnp.float32) * (E ** -0.5)
    dist = jnp.abs(pqb[..., :, None] - pks[..., None, :]).astype(jnp.float32)
    sc = sc - slopes[None, None, :, None, None, None] * dist[:, :, None]
    sc = jnp.where(valid, sc, NEG_INF)
    m = jnp.max(sc, axis=-1)
    p = jnp.exp(sc - m[..., None])
    s = jnp.sum(p, axis=-1)
    o = jnp.einsum('bdhnqk,bdhnke->bdhnqe', p, vs.astype(jnp.float32)) / s[..., None]
    o = o.reshape(B, dilation, H, Lp, E)[:, :, :, :L].transpose(0, 3, 1, 2, 4).reshape(B, S, H, E)
    m = m.reshape(B, dilation, H, Lp)[..., :L].transpose(0, 3, 1, 2).reshape(B, S, H)
    s = s.reshape(B, dilation, H, Lp)[..., :L].transpose(0, 3, 1, 2).reshape(B, S, H)
    return o, m, s


def dilated_attention(q, k, v, pos, slopes):
    outs = [dilated_branch(q, k, v, pos, slopes, d, w // 2 // d) for (w, d) in DILATED_PATTERNS]
    m_all = jnp.max(jnp.stack([m for (_, m, _) in outs]), axis=0)
    wts = [s * jnp.exp(m - m_all) for (_, m, s) in outs]
    num = sum(w[..., None] * o for w, (o, _, _) in zip(wts, outs))
    return num / sum(wts)[..., None]


def mla_attention(q, k, v):
    B, H, S, Dk = q.shape
    nq = S // QUERY_BLOCK
    scale = Dk ** -0.5
    qb = q.reshape(B, H, nq, QUERY_BLOCK, Dk).transpose(2, 0, 1, 3, 4)
    vf = v.astype(jnp.float32)

    def one(qi):
        sc = jnp.einsum('bhqe,bhke->bhqk', qi, k).astype(jnp.float32) * scale
        return jnp.einsum('bhqk,bhkv->bhqv', jax.nn.softmax(sc, axis=-1), vf)

    o = lax.map(one, qb)
    return o.transpose(1, 0, 3, 2, 4).reshape(B, S, H, v.shape[-1])


def diff_attention(q, k, v, pos, slopes, lam):
    B, H, _, S, E = q.shape
    nq = S // QUERY_BLOCK
    scale = E ** -0.5
    qb = q.reshape(B, H, 2, nq, QUERY_BLOCK, E).transpose(3, 0, 1, 2, 4, 5)
    pqb = pos.reshape(B, nq, QUERY_BLOCK).transpose(1, 0, 2)
    vf = v.astype(jnp.float32)

    def one(args):
        qi, pi = args
        sc = jnp.einsum('bhcqe,bhcke->bhcqk', qi, k).astype(jnp.float32) * scale
        dist = jnp.abs(pi[:, :, None] - pos[:, None, :]).astype(jnp.float32)
        sc = sc - slopes[None, :, None, None, None] * dist[:, None, None]
        a = jax.nn.softmax(sc, axis=-1)
        return jnp.einsum('bhqk,bhkv->bhqv', a[:, :, 0] - lam * a[:, :, 1], vf)

    o = lax.map(one, (qb, pqb))
    return o.transpose(1, 0, 3, 2, 4).reshape(B, S, H, v.shape[-1])


def token_mixing(h, pos, w_in, mla_q_norm, mla_w_uq, mla_kv_norm, mla_w_ukv,
                 lq1, lk1, lq2, lk2, diff_head_norm, w_out, layer_idx):
    B, S, _ = h.shape
    slopes_a, slopes_c = alibi_slopes()
    proj = h @ w_in
    a_part, b_part, c_part = jnp.split(proj, [A_COLS, A_COLS + B_COLS], axis=-1)

    qa, ka, va = [t.reshape(B, S, A_HEADS, HEAD_DIM) for t in jnp.split(a_part, 3, axis=-1)]
    out_a = dilated_attention(qa, ka, va, pos, slopes_a).astype(h.dtype)

    c_q, c_kv, k_pe = jnp.split(b_part, [MLA_Q_LORA, MLA_Q_LORA + MLA_KV_LORA], axis=-1)
    qfull = (rmsnorm(c_q, mla_q_norm) @ mla_w_uq).reshape(B, S, B_HEADS, MLA_NOPE + MLA_ROPE)
    q_b = jnp.concatenate([qfull[..., :MLA_NOPE], rope(qfull[..., MLA_NOPE:], pos)], axis=-1)
    kv = (rmsnorm(c_kv, mla_kv_norm) @ mla_w_ukv).reshape(B, S, B_HEADS, MLA_NOPE + MLA_V)
    k_rot = jnp.broadcast_to(rope(k_pe[:, :, None, :], pos), (B, S, B_HEADS, MLA_ROPE))
    k_b = jnp.concatenate([kv[..., :MLA_NOPE], k_rot], axis=-1)
    v_b = kv[..., MLA_NOPE:]
    out_b = mla_attention(q_b.transpose(0, 2, 1, 3), k_b.transpose(0, 2, 1, 3),
                          v_b.transpose(0, 2, 1, 3)).astype(h.dtype)

    nqk = C_HEADS * 2 * DIFF_QK
    q_c, k_c, v_c = jnp.split(c_part, [nqk, 2 * nqk], axis=-1)
    q_c = q_c.reshape(B, S, C_HEADS, 2, DIFF_QK).transpose(0, 2, 3, 1, 4)
    k_c = k_c.reshape(B, S, C_HEADS, 2, DIFF_QK).transpose(0, 2, 3, 1, 4)
    v_c = v_c.reshape(B, S, C_HEADS, DIFF_V).transpose(0, 2, 1, 3)
    lam_init = 0.8 - 0.6 * math.exp(-0.3 * layer_idx)
    lam = (jnp.exp(jnp.sum(lq1.astype(jnp.float32) * lk1.astype(jnp.float32)))
           - jnp.exp(jnp.sum(lq2.astype(jnp.float32) * lk2.astype(jnp.float32))) + lam_init)
    o_c = diff_attention(q_c, k_c, v_c, pos, slopes_c, lam).astype(h.dtype)
    out_c = rmsnorm(o_c, diff_head_norm) * (1.0 - lam_init)

    mixed = jnp.concatenate([out_a.reshape(B, S, -1), out_b.reshape(B, S, -1),
                             out_c.reshape(B, S, -1)], axis=-1)
    return mixed @ w_out


def setup_inputs(seed: int = 0) -> dict:
    key = jax.random.key(seed)
    ks = iter(jax.random.split(key, 32))
    f32 = jnp.float32

    def dense(shape, fan_in):
        return jax.random.normal(next(ks), shape, f32) * (fan_in ** -0.5)

    def gain(shape):
        return 1.0 + 0.02 * jax.random.normal(next(ks), shape, f32)

    return {
        "x": jax.random.normal(next(ks), (BATCH, SEQ, D_MODEL), f32),
        "positions": jnp.broadcast_to(jnp.arange(SEQ, dtype=jnp.int32), (BATCH, SEQ)),
        "ffn1_norm": gain((DEPTH, D_MODEL)),
        "ffn1_w_gate": dense((DEPTH, D_MODEL, D_FF), D_MODEL),
        "ffn1_w_up": dense((DEPTH, D_MODEL, D_FF), D_MODEL),
        "ffn1_w_down": dense((DEPTH, D_FF, D_MODEL), D_FF),
        "mix_norm": gain((DEPTH, D_MODEL)),
        "w_in": dense((DEPTH, D_MODEL, N_IN), D_MODEL),
        "mla_q_norm": gain((DEPTH, MLA_Q_LORA)),
        "mla_w_uq": dense((DEPTH, MLA_Q_LORA, B_HEADS * (MLA_NOPE + MLA_ROPE)), MLA_Q_LORA),
        "mla_kv_norm": gain((DEPTH, MLA_KV_LORA)),
        "mla_w_ukv": dense((DEPTH, MLA_KV_LORA, B_HEADS * (MLA_NOPE + MLA_V)), MLA_KV_LORA),
        "diff_lambda_q1": 0.1 * jax.random.normal(next(ks), (DEPTH, DIFF_QK), f32),
        "diff_lambda_k1": 0.1 * jax.random.normal(next(ks), (DEPTH, DIFF_QK), f32),
        "diff_lambda_q2": 0.1 * jax.random.normal(next(ks), (DEPTH, DIFF_QK), f32),
        "diff_lambda_k2": 0.1 * jax.random.normal(next(ks), (DEPTH, DIFF_QK), f32),
        "diff_head_norm": gain((DEPTH, DIFF_V)),
        "w_out": dense((DEPTH, D_MIX, D_MODEL), D_MIX),
        "ffn2_norm": gain((DEPTH, D_MODEL)),
        "ffn2_w_gate": dense((DEPTH, D_MODEL, D_FF), D_MODEL),
        "ffn2_w_up": dense((DEPTH, D_MODEL, D_FF), D_MODEL),
        "ffn2_w_down": dense((DEPTH, D_FF, D_MODEL), D_FF),
        "final_norm": gain((D_MODEL,)),
    }


def reference(x, positions, ffn1_norm, ffn1_w_gate, ffn1_w_up, ffn1_w_down, mix_norm, w_in,
              mla_q_norm, mla_w_uq, mla_kv_norm, mla_w_ukv, diff_lambda_q1, diff_lambda_k1,
              diff_lambda_q2, diff_lambda_k2, diff_head_norm, w_out, ffn2_norm, ffn2_w_gate,
              ffn2_w_up, ffn2_w_down, final_norm):
    for l in range(DEPTH):
        x = x + 0.5 * swiglu(rmsnorm(x, ffn1_norm[l]), ffn1_w_gate[l], ffn1_w_up[l], ffn1_w_down[l])
        x = x + token_mixing(rmsnorm(x, mix_norm[l]), positions, w_in[l], mla_q_norm[l], mla_w_uq[l],
                             mla_kv_norm[l], mla_w_ukv[l], diff_lambda_q1[l], diff_lambda_k1[l],
                             diff_lambda_q2[l], diff_lambda_k2[l], diff_head_norm[l], w_out[l], l)
        x = x + 0.5 * swiglu(rmsnorm(x, ffn2_norm[l]), ffn2_w_gate[l], ffn2_w_up[l], ffn2_w_down[l])
    return rmsnorm(x, final_norm)
```

```python
import functools
import math

import numpy as np
import jax
import jax.numpy as jnp
from jax import lax
from jax.experimental import pallas as pl
from jax.experimental.pallas import tpu as pltpu

D_MODEL = 1024
HEAD_DIM = 64
A_HEADS = 4
DILATED_PATTERNS = ((128, 1), (512, 4), (2048, 16))
B_HEADS = 8
MLA_Q_LORA = 384
MLA_KV_LORA = 256
MLA_NOPE = 64
MLA_ROPE = 32
MLA_V = 64
ROPE_THETA = 10000.0
C_HEADS = 4
DIFF_QK = 32
DIFF_V = 64
D_FF = 2816
EPS = 1e-6
NEG_INF = -1e30

A_COLS = 3 * A_HEADS * HEAD_DIM
B_COLS = MLA_Q_LORA + MLA_KV_LORA + MLA_ROPE
C_COLS = 2 * (C_HEADS * 2 * DIFF_QK) + C_HEADS * DIFF_V
A_OUT = A_HEADS * HEAD_DIM
B_OUT = B_HEADS * MLA_V
C_OUT = C_HEADS * DIFF_V

LANES = 128
LOG2E = math.log2(math.e)
VMEM_LIMIT = 56 * 1024 * 1024

BF16 = jnp.bfloat16
F32 = jnp.float32

TOKEN_TILE = 512
QUERY_TILE = 256


def _alibi_slopes():
    n = A_HEADS + C_HEADS
    s = [2.0 ** (-8.0 * (i + 1) / n) for i in range(n)]
    return s[0::2], s[1::2]


def _rms(x, gain):
    return x * lax.rsqrt(jnp.mean(x * x, axis=-1, keepdims=True) + EPS) * gain


def _dot(a, b):
    return jnp.dot(a, b, preferred_element_type=F32)


def _dot_nt(a, b):
    return lax.dot_general(a, b, (((1,), (1,)), ((), ())), preferred_element_type=F32)


def _params(n_grid):
    return pltpu.CompilerParams(dimension_semantics=("arbitrary",) * n_grid,
                                vmem_limit_bytes=VMEM_LIMIT)


def _const_spec(shape):
    return pl.BlockSpec(shape, lambda *_: (0,) * len(shape), pipeline_mode=pl.Buffered(1))


def _trig_kernel(pos_ref, inv_ref, cos_ref, sin_ref):
    ang = pos_ref[...].astype(F32) * inv_ref[...]
    cos_ref[...] = jnp.cos(ang)
    sin_ref[...] = jnp.sin(ang)


def _rotary_tables(positions):
    half = MLA_ROPE // 2
    t = positions.size
    rep = LANES // half
    rows = t // rep
    pos_rep = jnp.repeat(positions.reshape(t), half).reshape(rows, LANES)
    inv = ROPE_THETA ** (-jnp.arange(half, dtype=F32) / half)
    inv_row = jnp.tile(inv, rep).reshape(1, LANES)
    tr = min(rows, 512)
    cos, sin = pl.pallas_call(
        _trig_kernel,
        grid=(rows // tr,),
        in_specs=[pl.BlockSpec((tr, LANES), lambda i: (i, 0)), _const_spec((1, LANES))],
        out_specs=[pl.BlockSpec((tr, LANES), lambda i: (i, 0))] * 2,
        out_shape=[jax.ShapeDtypeStruct((rows, LANES), F32)] * 2,
        compiler_params=_params(1),
        name="rotary_trig",
    )(pos_rep, inv_row)
    cos = cos.reshape(t, half)
    sin = sin.reshape(t, half)
    root = math.sqrt((MLA_NOPE + MLA_ROPE) ** -0.5 * LOG2E)
    ones = jnp.ones((t, MLA_NOPE), F32)
    zeros_n = jnp.zeros((t, MLA_NOPE), F32)
    zeros_p = jnp.zeros((t, LANES - MLA_NOPE - MLA_ROPE), F32)
    ctab = root * jnp.concatenate([ones, cos, cos, zeros_p], axis=1)
    stab = root * jnp.concatenate([zeros_n, -sin, sin, zeros_p], axis=1)
    return ctab, stab, root


def _ffn_kernel(*refs, with_mix, with_final):
    it = iter(refs)
    x_ref = next(it)
    if with_mix:
        oa_ref, ob_ref, oc_ref, woa_ref, wob_ref, woc_ref = (next(it) for _ in range(6))
    gain_ref, wg_ref, wu_ref, wd_ref = (next(it) for _ in range(4))
    if with_final:
        fgain_ref = next(it)
    o_ref = next(it)

    x = x_ref[...]
    if with_mix:
        x = x + _dot(oa_ref[...], woa_ref[...]) + _dot(ob_ref[...], wob_ref[...]) \
              + _dot(oc_ref[...], woc_ref[...])
    xn = _rms(x, gain_ref[...]).astype(BF16)
    g = _dot(xn, wg_ref[...])
    u = _dot(xn, wu_ref[...])
    h = (g * jax.nn.sigmoid(g) * u).astype(BF16)
    y = x + 0.5 * _dot(h, wd_ref[...])
    if with_final:
        y = _rms(y, fgain_ref[...])
    o_ref[...] = y


def _ffn(x, gain, wg, wu, wd, mix=None, final_gain=None):
    t, d = x.shape
    tm = min(TOKEN_TILE, t)
    row = lambda w: pl.BlockSpec((tm, w), lambda i: (i, 0))
    args, specs = [x], [row(d)]
    if mix is not None:
        oa, ob, oc, woa, wob, woc = mix
        args += [oa, ob, oc, woa, wob, woc]
        specs += [row(A_OUT), row(B_OUT), row(C_OUT),
                  _const_spec(woa.shape), _const_spec(wob.shape), _const_spec(woc.shape)]
    args += [gain.reshape(1, d), wg, wu, wd]
    specs += [_const_spec((1, d)), _const_spec(wg.shape), _const_spec(wu.shape), _const_spec(wd.shape)]
    if final_gain is not None:
        args.append(final_gain.reshape(1, d))
        specs.append(_const_spec((1, d)))
    return pl.pallas_call(
        functools.partial(_ffn_kernel, with_mix=mix is not None, with_final=final_gain is not None),
        grid=(t // tm,),
        in_specs=specs,
        out_specs=row(d),
        out_shape=jax.ShapeDtypeStruct((t, d), F32),
        compiler_params=_params(1),
        name="ffn",
    )(*args)


_P_QA, _P_KA, _P_VA = 0, A_OUT, 2 * A_OUT
_P_CQ = A_COLS
_P_CKV = _P_CQ + MLA_Q_LORA
_P_KPE = _P_CKV + MLA_KV_LORA
_P_KPE_SW = _P_KPE + LANES
_P_QC = _P_KPE_SW + LANES
_P_KC = _P_QC + C_HEADS * 2 * DIFF_QK
_P_VC = _P_KC + C_HEADS * 2 * DIFF_QK
_P_END = _P_VC + C_OUT
_QB_W = B_HEADS * LANES


def _proj_kernel(x_ref, gmix_ref, win_ref, gq_ref, wuq_ref, gkv_ref, wukv_ref, ctab_ref, stab_ref,
                 qa_ref, ka_ref, va_ref, qb_ref, kb_ref, vb_ref, qc_ref, kc_ref, vc_ref, *, root):
    xn = _rms(x_ref[...], gmix_ref[...]).astype(BF16)
    p = _dot(xn, win_ref[...])
    qa_ref[...] = (p[:, _P_QA:_P_KA] * (HEAD_DIM ** -0.5 * LOG2E)).astype(BF16)
    ka_ref[...] = p[:, _P_KA:_P_VA].astype(BF16)
    va_ref[...] = p[:, _P_VA:_P_CQ].astype(BF16)
    qc_ref[...] = (p[:, _P_QC:_P_KC] * (DIFF_QK ** -0.5 * LOG2E)).astype(BF16)
    kc_ref[...] = p[:, _P_KC:_P_VC].astype(BF16)
    vc_ref[...] = p[:, _P_VC:_P_END].astype(BF16)

    ctab = ctab_ref[...]
    stab = stab_ref[...]
    cq = _rms(p[:, _P_CQ:_P_CKV], gq_ref[...]).astype(BF16)
    qq = _dot(cq, wuq_ref[...])
    for h in range(B_HEADS):
        lo = h * LANES
        qb_ref[:, lo:lo + LANES] = (qq[:, lo:lo + LANES] * ctab
                                    + qq[:, _QB_W + lo:_QB_W + lo + LANES] * stab).astype(BF16)
    ckv = _rms(p[:, _P_CKV:_P_KPE], gkv_ref[...]).astype(BF16)
    kk = _dot(ckv, wukv_ref[...])
    krot = p[:, _P_KPE:_P_KPE_SW] * ctab + p[:, _P_KPE_SW:_P_QC] * stab
    for h in range(B_HEADS):
        lo = h * LANES
        kb_ref[:, lo:lo + LANES] = (kk[:, lo:lo + LANES] * root + krot).astype(BF16)
    vb_ref[...] = kk[:, _QB_W:_QB_W + B_OUT].astype(BF16)


def _rearranged_weights(w_in, w_uq, w_ukv):
    d = w_in.shape[0]
    half = MLA_ROPE // 2
    pad = LANES - MLA_NOPE - MLA_ROPE
    kpe = w_in[:, _P_KPE:_P_KPE + MLA_ROPE]
    z = lambda r, c: jnp.zeros((r, c), w_in.dtype)
    win = jnp.concatenate([
        w_in[:, :_P_KPE],
        z(d, MLA_NOPE), kpe, z(d, pad),
        z(d, MLA_NOPE), kpe[:, half:], kpe[:, :half], z(d, pad),
        w_in[:, _P_KPE + MLA_ROPE:]], axis=1)
    r = w_uq.shape[0]
    uq = w_uq.reshape(r, B_HEADS, MLA_NOPE + MLA_ROPE)
    nope, rope = uq[:, :, :MLA_NOPE], uq[:, :, MLA_NOPE:]
    zp = jnp.zeros((r, B_HEADS, pad), w_uq.dtype)
    zn = jnp.zeros((r, B_HEADS, MLA_NOPE), w_uq.dtype)
    main = jnp.concatenate([nope, rope, zp], axis=2).reshape(r, _QB_W)
    swap = jnp.concatenate([zn, rope[:, :, half:], rope[:, :, :half], zp], axis=2).reshape(r, _QB_W)
    wuq = jnp.concatenate([main, swap], axis=1)
    r = w_ukv.shape[0]
    ukv = w_ukv.reshape(r, B_HEADS, MLA_NOPE + MLA_V)
    kn = jnp.concatenate([ukv[:, :, :MLA_NOPE], jnp.zeros((r, B_HEADS, LANES - MLA_NOPE), w_ukv.dtype)],
                         axis=2).reshape(r, _QB_W)
    wukv = jnp.concatenate([kn, ukv[:, :, MLA_NOPE:].reshape(r, B_OUT)], axis=1)
    return win.astype(BF16), wuq.astype(BF16), wukv.astype(BF16)


def _project(x, gmix, win, gq, wuq, gkv, wukv, ctab, stab, root):
    t, d = x.shape
    tm = min(TOKEN_TILE, t)
    row = lambda w: pl.BlockSpec((tm, w), lambda i: (i, 0))
    widths = [A_OUT, A_OUT, A_OUT, _QB_W, _QB_W, B_OUT, C_HEADS * 2 * DIFF_QK, C_HEADS * 2 * DIFF_QK, C_OUT]
    return pl.pallas_call(
        functools.partial(_proj_kernel, root=root),
        grid=(t // tm,),
        in_specs=[row(d), _const_spec((1, d)), _const_spec(win.shape),
                  _const_spec((1, MLA_Q_LORA)), _const_spec(wuq.shape),
                  _const_spec((1, MLA_KV_LORA)), _const_spec(wukv.shape),
                  row(LANES), row(LANES)],
        out_specs=[row(w) for w in widths],
        out_shape=[jax.ShapeDtypeStruct((t, w), BF16) for w in widths],
        compiler_params=_params(1),
        name="mixer_proj",
    )(x, gmix.reshape(1, d), win, gq.reshape(1, -1), wuq, gkv.reshape(1, -1), wukv, ctab, stab)


def _softmax_pv(s, v_aug):
    m = jnp.max(s, axis=-1, keepdims=True)
    p = jnp.exp2(s - m).astype(BF16)
    o = _dot(p, v_aug)
    return o[:, :LANES], o[:, LANES:LANES + 1]


def _fill_v_aug(vaug_ref, v):
    vaug_ref[:, :LANES] = v
    vaug_ref[:, LANES:] = jnp.ones_like(v)


def _lane_ids(rows):
    return lax.broadcasted_iota(jnp.int32, (rows, LANES), 1)


def _dilated_log2_weight_table(tq, s):
    nq = s // tq
    cols = (2 * nq - 1) * tq
    delta = np.arange(cols)[None, :] - (nq - 1) * tq - np.arange(tq)[:, None]
    mult = np.zeros(delta.shape, np.int64)
    for window, dil in DILATED_PATTERNS:
        radius = window // 2 // dil
        mult += ((delta % dil == 0) & (np.abs(delta) <= radius * dil)).astype(np.int64)
    table = np.full(delta.shape, NEG_INF, np.float32)
    table[mult > 0] = np.log2(mult[mult > 0]).astype(np.float32)
    return jnp.asarray(table)


def _dilated_kernel(q_ref, k_ref, v_ref, pq_ref, pk_ref, tab_ref, o_ref, vaug_ref, *, tq, s, slopes):
    qi = pl.program_id(1)
    nq = s // tq
    dist = jnp.abs(pq_ref[...] - pk_ref[0]).astype(F32)
    start = pl.multiple_of((nq - 1 - qi) * tq, LANES)
    bias = tab_ref[:, pl.ds(start, s)]
    lane = _lane_ids(tq)

    @pl.when(qi == 0)
    def _():
        for pair in range(A_HEADS // 2):
            _fill_v_aug(vaug_ref.at[pair], v_ref[:, pair * LANES:(pair + 1) * LANES])

    for pair in range(A_HEADS // 2):
        lo = pair * LANES
        q = q_ref[:, lo:lo + LANES]
        k = k_ref[:, lo:lo + LANES]
        outs = []
        for i in range(2):
            sel = (lane >= i * HEAD_DIM) & (lane < (i + 1) * HEAD_DIM)
            sc = _dot_nt(jnp.where(sel, q, jnp.zeros_like(q)), k)
            sc = sc - (slopes[2 * pair + i] * LOG2E) * dist + bias
            acc, rowsum = _softmax_pv(sc, vaug_ref[pair])
            outs.append(acc / rowsum)
        o_ref[:, lo:lo + LANES] = jnp.where(lane < HEAD_DIM, outs[0], outs[1]).astype(BF16)


def _dilated_attention(qa, ka, va, pos_col, pos_row, batch, s):
    t = qa.shape[0]
    tq = min(QUERY_TILE, s)
    nq = s // tq
    table = _dilated_log2_weight_table(tq, s)
    slopes, _ = _alibi_slopes()
    return pl.pallas_call(
        functools.partial(_dilated_kernel, tq=tq, s=s, slopes=slopes),
        grid=(batch, nq),
        in_specs=[pl.BlockSpec((tq, A_OUT), lambda b, i: (b * nq + i, 0)),
                  pl.BlockSpec((s, A_OUT), lambda b, i: (b, 0)),
                  pl.BlockSpec((s, A_OUT), lambda b, i: (b, 0)),
                  pl.BlockSpec((tq, 1), lambda b, i: (b * nq + i, 0)),
                  pl.BlockSpec((1, 1, s), lambda b, i: (b, 0, 0)),
                  _const_spec(table.shape)],
        out_specs=pl.BlockSpec((tq, A_OUT), lambda b, i: (b * nq + i, 0)),
        out_shape=jax.ShapeDtypeStruct((t, A_OUT), BF16),
        scratch_shapes=[pltpu.VMEM((A_HEADS // 2, s, 2 * LANES), BF16)],
        compiler_params=_params(2),
        name="dilated_attn",
    )(qa, ka, va, pos_col, pos_row, table)


def _mla_kernel(q_ref, k_ref, v_ref, o_ref, vaug_ref):
    @pl.when(pl.program_id(2) == 0)
    def _():
        _fill_v_aug(vaug_ref, v_ref[...])

    lane = _lane_ids(q_ref.shape[0])
    outs = []
    for i in range(2):
        lo = i * LANES
        sc = _dot_nt(q_ref[:, lo:lo + LANES], k_ref[:, lo:lo + LANES])
        acc, rowsum = _softmax_pv(sc, vaug_ref[...])
        outs.append(acc / rowsum)
    o_ref[...] = jnp.where(lane < MLA_V, outs[0], outs[1]).astype(BF16)


def _mla_attention(qb, kb, vb, batch, s):
    t = qb.shape[0]
    tq = min(QUERY_TILE, s)
    nq = s // tq
    pairs = B_HEADS // 2
    return pl.pallas_call(
        _mla_kernel,
        grid=(batch, pairs, nq),
        in_specs=[pl.BlockSpec((tq, 2 * LANES), lambda b, h, i: (b * nq + i, h)),
                  pl.BlockSpec((s, 2 * LANES), lambda b, h, i: (b, h)),
                  pl.BlockSpec((s, LANES), lambda b, h, i: (b, h))],
        out_specs=pl.BlockSpec((tq, LANES), lambda b, h, i: (b * nq + i, h)),
        out_shape=jax.ShapeDtypeStruct((t, B_OUT), BF16),
        scratch_shapes=[pltpu.VMEM((s, 2 * LANES), BF16)],
        compiler_params=_params(3),
        name="mla_attn",
    )(qb, kb, vb)


def _diff_kernel(q_ref, k_ref, v_ref, pq_ref, pk_ref, lq1_ref, lk1_ref, lq2_ref, lk2_ref, gain_ref,
                 o_ref, vaug_ref, *, tq, slopes, lam_init):
    dist = jnp.abs(pq_ref[...] - pk_ref[0]).astype(F32)
    lam = (jnp.exp(jnp.sum(lq1_ref[...] * lk1_ref[...], axis=-1, keepdims=True))
           - jnp.exp(jnp.sum(lq2_ref[...] * lk2_ref[...], axis=-1, keepdims=True)) + lam_init)
    lane = _lane_ids(tq)
    gain = gain_ref[...]

    @pl.when(pl.program_id(1) == 0)
    def _():
        for pair in range(C_HEADS // 2):
            _fill_v_aug(vaug_ref.at[pair], v_ref[:, pair * LANES:(pair + 1) * LANES])

    for pair in range(C_HEADS // 2):
        lo = pair * LANES
        q = q_ref[:, lo:lo + LANES]
        k = k_ref[:, lo:lo + LANES]
        outs = []
        for i in range(2):
            sdist = (slopes[2 * pair + i] * LOG2E) * dist
            maps = []
            for c in range(2):
                first = (2 * i + c) * DIFF_QK
                sel = (lane >= first) & (lane < first + DIFF_QK)
                sc = _dot_nt(jnp.where(sel, q, jnp.zeros_like(q)), k) - sdist
                acc, rowsum = _softmax_pv(sc, vaug_ref[pair])
                maps.append(acc / rowsum)
            outs.append(maps[0] - lam * maps[1])
        o = jnp.where(lane < DIFF_V, outs[0], outs[1])
        sq = o * o
        ms0 = jnp.sum(jnp.where(lane < DIFF_V, sq, 0.0), axis=-1, keepdims=True) / DIFF_V
        ms1 = jnp.sum(jnp.where(lane < DIFF_V, 0.0, sq), axis=-1, keepdims=True) / DIFF_V
        ms = jnp.where(lane < DIFF_V, ms0, ms1)
        o_ref[:, lo:lo + LANES] = (o * lax.rsqrt(ms + EPS) * gain * (1.0 - lam_init)).astype(BF16)


def _diff_attention(qc, kc, vc, pos_col, pos_row, lq1, lk1, lq2, lk2, head_gain, layer_idx, batch, s):
    t = qc.shape[0]
    tq = min(QUERY_TILE, s)
    nq = s // tq
    _, slopes = _alibi_slopes()
    lam_init = 0.8 - 0.6 * math.exp(-0.3 * layer_idx)
    w = C_HEADS * 2 * DIFF_QK
    vec = lambda a: a.reshape(1, DIFF_QK)
    gain2 = jnp.concatenate([head_gain, head_gain]).reshape(1, LANES)
    return pl.pallas_call(
        functools.partial(_diff_kernel, tq=tq, slopes=slopes, lam_init=lam_init),
        grid=(batch, nq),
        in_specs=[pl.BlockSpec((tq, w), lambda b, i: (b * nq + i, 0)),
                  pl.BlockSpec((s, w), lambda b, i: (b, 0)),
                  pl.BlockSpec((s, C_OUT), lambda b, i: (b, 0)),
                  pl.BlockSpec((tq, 1), lambda b, i: (b * nq + i, 0)),
                  pl.BlockSpec((1, 1, s), lambda b, i: (b, 0, 0)),
                  _const_spec((1, DIFF_QK)), _const_spec((1, DIFF_QK)),
                  _const_spec((1, DIFF_QK)), _const_spec((1, DIFF_QK)),
                  _const_spec((1, LANES))],
        out_specs=pl.BlockSpec((tq, C_OUT), lambda b, i: (b * nq + i, 0)),
        out_shape=jax.ShapeDtypeStruct((t, C_OUT), BF16),
        scratch_shapes=[pltpu.VMEM((C_HEADS // 2, s, 2 * LANES), BF16)],
        compiler_params=_params(2),
        name="diff_attn",
    )(qc, kc, vc, pos_col, pos_row, vec(lq1), vec(lk1), vec(lq2), vec(lk2), gain2)


def kernel(x, positions, ffn1_norm, ffn1_w_gate, ffn1_w_up, ffn1_w_down, mix_norm, w_in, mla_q_norm, mla_w_uq, mla_kv_norm, mla_w_ukv, diff_lambda_q1, diff_lambda_k1, diff_lambda_q2, diff_lambda_k2, diff_head_norm, w_out, ffn2_norm, ffn2_w_gate, ffn2_w_up, ffn2_w_down, final_norm):
    batch, s, d = x.shape
    t = batch * s
    depth = w_in.shape[0]
    h = x.reshape(t, d)
    pos_col = positions.reshape(t, 1)
    pos_row = positions.reshape(batch, 1, s)
    ctab, stab, root = _rotary_tables(positions)
    bf = lambda w: w.astype(BF16)

    for l in range(depth):
        h = _ffn(h, ffn1_norm[l], bf(ffn1_w_gate[l]), bf(ffn1_w_up[l]), bf(ffn1_w_down[l]))
        win, wuq, wukv = _rearranged_weights(w_in[l], mla_w_uq[l], mla_w_ukv[l])
        qa, ka, va, qb, kb, vb, qc, kc, vc = _project(
            h, mix_norm[l], win, mla_q_norm[l], wuq, mla_kv_norm[l], wukv, ctab, stab, root)
        out_a = _dilated_attention(qa, ka, va, pos_col, pos_row, batch, s)
        out_b = _mla_attention(qb, kb, vb, batch, s)
        out_c = _diff_attention(qc, kc, vc, pos_col, pos_row, diff_lambda_q1[l], diff_lambda_k1[l],
                                diff_lambda_q2[l], diff_lambda_k2[l], diff_head_norm[l], l, batch, s)
        wo = bf(w_out[l])
        mix = (out_a, out_b, out_c, wo[:A_OUT], wo[A_OUT:A_OUT + B_OUT], wo[A_OUT + B_OUT:])
        h = _ffn(h, ffn2_norm[l], bf(ffn2_w_gate[l]), bf(ffn2_w_up[l]), bf(ffn2_w_down[l]),
                 mix=mix, final_gain=final_norm if l == depth - 1 else None)
    return h.reshape(batch, s, d)
```

```python
import functools
import math

import numpy as np
import jax
import jax.numpy as jnp
from jax import lax
from jax.experimental import pallas as pl
from jax.experimental.pallas import tpu as pltpu

D_MODEL = 1024
HEAD_DIM = 64
A_HEADS = 4
DILATED_PATTERNS = ((128, 1), (512, 4), (2048, 16))
B_HEADS = 8
MLA_Q_LORA = 384
MLA_KV_LORA = 256
MLA_NOPE = 64
MLA_ROPE = 32
MLA_V = 64
ROPE_THETA = 10000.0
C_HEADS = 4
DIFF_QK = 32
DIFF_V = 64
D_FF = 2816
EPS = 1e-6
NEG_INF = -1e30

A_COLS = 3 * A_HEADS * HEAD_DIM
B_COLS = MLA_Q_LORA + MLA_KV_LORA + MLA_ROPE
C_COLS = 2 * (C_HEADS * 2 * DIFF_QK) + C_HEADS * DIFF_V
A_OUT = A_HEADS * HEAD_DIM
B_OUT = B_HEADS * MLA_V
C_OUT = C_HEADS * DIFF_V

LANES = 128
SUBLANES = 8
LOG2E = math.log2(math.e)
VMEM_LIMIT = 56 * 1024 * 1024

BF16 = jnp.bfloat16
F32 = jnp.float32

TOKEN_TILE = 512
QUERY_TILE = 256


def _alibi_slopes():
    n = A_HEADS + C_HEADS
    s = [2.0 ** (-8.0 * (i + 1) / n) for i in range(n)]
    return s[0::2], s[1::2]


def _rms(x, gain):
    return x * lax.rsqrt(jnp.mean(x * x, axis=-1, keepdims=True) + EPS) * gain


def _dot(a, b):
    return jnp.dot(a, b, preferred_element_type=F32)


def _dot_nt(a, b):
    return lax.dot_general(a, b, (((1,), (1,)), ((), ())), preferred_element_type=F32)


def _params(n_grid):
    return pltpu.CompilerParams(dimension_semantics=("arbitrary",) * n_grid,
                                vmem_limit_bytes=VMEM_LIMIT)


def _const_spec(shape):
    return pl.BlockSpec(shape, lambda *_: (0,) * len(shape), pipeline_mode=pl.Buffered(1))


def _trig_kernel(pos_ref, inv_ref, cos_ref, sin_ref):
    ang = pos_ref[...].astype(F32) * inv_ref[...]
    cos_ref[...] = jnp.cos(ang)
    sin_ref[...] = jnp.sin(ang)


def _rotary_tables(positions):
    half = MLA_ROPE // 2
    t = positions.size
    rep = LANES // half
    rows = t // rep
    pos_rep = jnp.repeat(positions.reshape(t), half).reshape(rows, LANES)
    inv = ROPE_THETA ** (-jnp.arange(half, dtype=F32) / half)
    inv_row = jnp.tile(inv, rep).reshape(1, LANES)
    tr = min(rows, 512)
    cos, sin = pl.pallas_call(
        _trig_kernel,
        grid=(rows // tr,),
        in_specs=[pl.BlockSpec((tr, LANES), lambda i: (i, 0)), _const_spec((1, LANES))],
        out_specs=[pl.BlockSpec((tr, LANES), lambda i: (i, 0))] * 2,
        out_shape=[jax.ShapeDtypeStruct((rows, LANES), F32)] * 2,
        compiler_params=_params(1),
        name="rotary_trig",
    )(pos_rep, inv_row)
    cos = cos.reshape(t, half)
    sin = sin.reshape(t, half)
    root = math.sqrt((MLA_NOPE + MLA_ROPE) ** -0.5 * LOG2E)
    ones = jnp.ones((t, MLA_NOPE), F32)
    zeros_n = jnp.zeros((t, MLA_NOPE), F32)
    zeros_p = jnp.zeros((t, LANES - MLA_NOPE - MLA_ROPE), F32)
    ctab = root * jnp.concatenate([ones, cos, cos, zeros_p], axis=1)
    stab = root * jnp.concatenate([zeros_n, -sin, sin, zeros_p], axis=1)
    return ctab, stab, root


def _ffn_kernel(*refs, with_mix, with_final):
    it = iter(refs)
    x_ref = next(it)
    if with_mix:
        oa_ref, ob_ref, oc_ref, woa_ref, wob_ref, woc_ref = (next(it) for _ in range(6))
    gain_ref, wg_ref, wu_ref, wd_ref = (next(it) for _ in range(4))
    if with_final:
        fgain_ref = next(it)
    o_ref = next(it)

    x = x_ref[...]
    if with_mix:
        x = x + _dot(oa_ref[...], woa_ref[...]) + _dot(ob_ref[...], wob_ref[...]) \
              + _dot(oc_ref[...], woc_ref[...])
    xn = _rms(x, gain_ref[...]).astype(BF16)
    g = _dot(xn, wg_ref[...])
    u = _dot(xn, wu_ref[...])
    h = (g * jax.nn.sigmoid(g) * u).astype(BF16)
    y = x + 0.5 * _dot(h, wd_ref[...])
    if with_final:
        y = _rms(y, fgain_ref[...])
    o_ref[...] = y


def _ffn(x, gain, wg, wu, wd, mix=None, final_gain=None):
    t, d = x.shape
    tm = min(TOKEN_TILE, t)
    row = lambda w: pl.BlockSpec((tm, w), lambda i: (i, 0))
    args, specs = [x], [row(d)]
    if mix is not None:
        oa, ob, oc, woa, wob, woc = mix
        args += [oa, ob, oc, woa, wob, woc]
        specs += [row(A_OUT), row(B_OUT), row(C_OUT),
                  _const_spec(woa.shape), _const_spec(wob.shape), _const_spec(woc.shape)]
    args += [gain.reshape(1, d), wg, wu, wd]
    specs += [_const_spec((1, d)), _const_spec(wg.shape), _const_spec(wu.shape), _const_spec(wd.shape)]
    if final_gain is not None:
        args.append(final_gain.reshape(1, d))
        specs.append(_const_spec((1, d)))
    return pl.pallas_call(
        functools.partial(_ffn_kernel, with_mix=mix is not None, with_final=final_gain is not None),
        grid=(t // tm,),
        in_specs=specs,
        out_specs=row(d),
        out_shape=jax.ShapeDtypeStruct((t, d), F32),
        compiler_params=_params(1),
        name="ffn",
    )(*args)


_P_QA, _P_KA, _P_VA = 0, A_OUT, 2 * A_OUT
_P_CQ = A_COLS
_P_CKV = _P_CQ + MLA_Q_LORA
_P_KPE = _P_CKV + MLA_KV_LORA
_P_KPE_SW = _P_KPE + LANES
_P_QC = _P_KPE_SW + LANES
_P_KC = _P_QC + C_HEADS * 2 * DIFF_QK
_P_VC = _P_KC + C_HEADS * 2 * DIFF_QK
_P_END = _P_VC + C_OUT
_QB_W = B_HEADS * LANES


def _proj_kernel(x_ref, gmix_ref, win_ref, gq_ref, wuq_ref, gkv_ref, wukv_ref, ctab_ref, stab_ref,
                 qa_ref, ka_ref, va_ref, qb_ref, kb_ref, vb_ref, qc_ref, kc_ref, vc_ref, *, root):
    xn = _rms(x_ref[...], gmix_ref[...]).astype(BF16)
    p = _dot(xn, win_ref[...])
    qa_ref[...] = (p[:, _P_QA:_P_KA] * (HEAD_DIM ** -0.5 * LOG2E)).astype(BF16)
    ka_ref[...] = p[:, _P_KA:_P_VA].astype(BF16)
    va_ref[...] = p[:, _P_VA:_P_CQ].astype(BF16)
    qc_ref[...] = (p[:, _P_QC:_P_KC] * (DIFF_QK ** -0.5 * LOG2E)).astype(BF16)
    kc_ref[...] = p[:, _P_KC:_P_VC].astype(BF16)
    vc_ref[...] = p[:, _P_VC:_P_END].astype(BF16)

    ctab = ctab_ref[...]
    stab = stab_ref[...]
    cq = _rms(p[:, _P_CQ:_P_CKV], gq_ref[...]).astype(BF16)
    qq = _dot(cq, wuq_ref[...])
    for h in range(B_HEADS):
        lo = h * LANES
        qb_ref[:, lo:lo + LANES] = (qq[:, lo:lo + LANES] * ctab
                                    + qq[:, _QB_W + lo:_QB_W + lo + LANES] * stab).astype(BF16)
    ckv = _rms(p[:, _P_CKV:_P_KPE], gkv_ref[...]).astype(BF16)
    kk = _dot(ckv, wukv_ref[...])
    krot = p[:, _P_KPE:_P_KPE_SW] * ctab + p[:, _P_KPE_SW:_P_QC] * stab
    for h in range(B_HEADS):
        lo = h * LANES
        kb_ref[:, lo:lo + LANES] = (kk[:, lo:lo + LANES] * root + krot).astype(BF16)
    vb_ref[...] = kk[:, _QB_W:_QB_W + B_OUT].astype(BF16)


def _rearranged_weights(w_in, w_uq, w_ukv):
    d = w_in.shape[0]
    half = MLA_ROPE // 2
    pad = LANES - MLA_NOPE - MLA_ROPE
    kpe = w_in[:, _P_KPE:_P_KPE + MLA_ROPE]
    z = lambda r, c: jnp.zeros((r, c), w_in.dtype)
    win = jnp.concatenate([
        w_in[:, :_P_KPE],
        z(d, MLA_NOPE), kpe, z(d, pad),
        z(d, MLA_NOPE), kpe[:, half:], kpe[:, :half], z(d, pad),
        w_in[:, _P_KPE + MLA_ROPE:]], axis=1)
    r = w_uq.shape[0]
    uq = w_uq.reshape(r, B_HEADS, MLA_NOPE + MLA_ROPE)
    nope, rope = uq[:, :, :MLA_NOPE], uq[:, :, MLA_NOPE:]
    zp = jnp.zeros((r, B_HEADS, pad), w_uq.dtype)
    zn = jnp.zeros((r, B_HEADS, MLA_NOPE), w_uq.dtype)
    main = jnp.concatenate([nope, rope, zp], axis=2).reshape(r, _QB_W)
    swap = jnp.concatenate([zn, rope[:, :, half:], rope[:, :, :half], zp], axis=2).reshape(r, _QB_W)
    wuq = jnp.concatenate([main, swap], axis=1)
    r = w_ukv.shape[0]
    ukv = w_ukv.reshape(r, B_HEADS, MLA_NOPE + MLA_V)
    kn = jnp.concatenate([ukv[:, :, :MLA_NOPE], jnp.zeros((r, B_HEADS, LANES - MLA_NOPE), w_ukv.dtype)],
                         axis=2).reshape(r, _QB_W)
    wukv = jnp.concatenate([kn, ukv[:, :, MLA_NOPE:].reshape(r, B_OUT)], axis=1)
    return win.astype(BF16), wuq.astype(BF16), wukv.astype(BF16)


def _project(x, gmix, win, gq, wuq, gkv, wukv, ctab, stab, root):
    t, d = x.shape
    tm = min(TOKEN_TILE, t)
    row = lambda w: pl.BlockSpec((tm, w), lambda i: (i, 0))
    widths = [A_OUT, A_OUT, A_OUT, _QB_W, _QB_W, B_OUT, C_HEADS * 2 * DIFF_QK, C_HEADS * 2 * DIFF_QK, C_OUT]
    return pl.pallas_call(
        functools.partial(_proj_kernel, root=root),
        grid=(t // tm,),
        in_specs=[row(d), _const_spec((1, d)), _const_spec(win.shape),
                  _const_spec((1, MLA_Q_LORA)), _const_spec(wuq.shape),
                  _const_spec((1, MLA_KV_LORA)), _const_spec(wukv.shape),
                  row(LANES), row(LANES)],
        out_specs=[row(w) for w in widths],
        out_shape=[jax.ShapeDtypeStruct((t, w), BF16) for w in widths],
        compiler_params=_params(1),
        name="mixer_proj",
    )(x, gmix.reshape(1, d), win, gq.reshape(1, -1), wuq, gkv.reshape(1, -1), wukv, ctab, stab)


UNDERFLOW_GUARD = 2.0 ** -40


def _softmax_pv(s, c, v_aug):
    p = jnp.exp2(s - c).astype(BF16)
    o = _dot(p, v_aug)
    return o[:, :LANES], o[:, LANES:LANES + 1]


def _fill_v_aug(vaug_ref, v):
    vaug_ref[:, :LANES] = v
    vaug_ref[:, LANES:] = jnp.ones_like(v)


def _lane_ids(rows):
    return lax.broadcasted_iota(jnp.int32, (rows, LANES), 1)


def _window(lane, first, width):
    return (lane >= first) & (lane < first + width)


def _store_max_key_norm(kmax_ref, row, k, first, width):
    kf = k.astype(F32)
    sq = kf * kf
    if width < LANES:
        sq = jnp.where(_window(_lane_ids(k.shape[0]), first, width), sq, 0.0)
    norm = jnp.max(jnp.sum(sq, axis=-1, keepdims=True), axis=0, keepdims=True)
    kmax_ref[row:row + 1, :] = jnp.broadcast_to(norm, (1, LANES))


def _score_bound(qm, kmax_ref, row):
    qf = qm.astype(F32)
    return jnp.sqrt(jnp.sum(qf * qf, axis=-1, keepdims=True) * kmax_ref[row:row + 1, 0:1])


def _fast_then_exact(run, write):
    outs, sums = run(False)
    write(outs)
    low = functools.reduce(jnp.minimum, sums)
    safe = jnp.min(low) >= UNDERFLOW_GUARD

    @pl.when(jnp.logical_not(safe))
    def _():
        write(run(True)[0])


def _dilated_log2_weight_table(tq, s):
    nq = s // tq
    cols = (2 * nq - 1) * tq
    delta = np.arange(cols)[None, :] - (nq - 1) * tq - np.arange(tq)[:, None]
    mult = np.zeros(delta.shape, np.int64)
    for window, dil in DILATED_PATTERNS:
        radius = window // 2 // dil
        mult += ((delta % dil == 0) & (np.abs(delta) <= radius * dil)).astype(np.int64)
    table = np.full(delta.shape, NEG_INF, np.float32)
    table[mult > 0] = np.log2(mult[mult > 0]).astype(np.float32)
    return jnp.asarray(table)


_MAX_LOG2_MULT = math.log2(len(DILATED_PATTERNS))


def _dilated_kernel(q_ref, k_ref, v_ref, pq_ref, pk_ref, tab_ref, o_ref, vaug_ref, kmax_ref, *, tq, s, slopes):
    qi = pl.program_id(1)
    nq = s // tq
    lane = _lane_ids(tq)

    @pl.when(qi == 0)
    def _():
        for pair in range(A_HEADS // 2):
            _fill_v_aug(vaug_ref.at[pair], v_ref[:, pair * LANES:(pair + 1) * LANES])
            for i in range(2):
                _store_max_key_norm(kmax_ref, 2 * pair + i, k_ref[:, pair * LANES:(pair + 1) * LANES],
                                    i * HEAD_DIM, HEAD_DIM)

    dist = jnp.abs(pq_ref[...] - pk_ref[0]).astype(F32)
    start = pl.multiple_of((nq - 1 - qi) * tq, LANES)

    def run(exact):
        outs, sums = [], []
        for head in range(A_HEADS):
            pair, i = divmod(head, 2)
            lo = pair * LANES
            q = q_ref[:, lo:lo + LANES]
            qm = jnp.where(_window(lane, i * HEAD_DIM, HEAD_DIM), q, jnp.zeros_like(q))
            sc = _dot_nt(qm, k_ref[:, lo:lo + LANES])
            sc = sc - (slopes[head] * LOG2E) * dist + tab_ref[:, pl.ds(start, s)]
            if exact:
                c = jnp.max(sc, axis=-1, keepdims=True)
            else:
                c = _score_bound(qm, kmax_ref, head) + _MAX_LOG2_MULT
            acc, rowsum = _softmax_pv(sc, c, vaug_ref[pair])
            outs.append(acc / rowsum)
            sums.append(rowsum)
        return outs, sums

    def write(outs):
        for pair in range(A_HEADS // 2):
            o_ref[:, pair * LANES:(pair + 1) * LANES] = jnp.where(
                lane < HEAD_DIM, outs[2 * pair], outs[2 * pair + 1]).astype(BF16)

    _fast_then_exact(run, write)


def _dilated_attention(qa, ka, va, pos_col, pos_row, batch, s):
    t = qa.shape[0]
    tq = min(QUERY_TILE, s)
    nq = s // tq
    table = _dilated_log2_weight_table(tq, s)
    slopes, _ = _alibi_slopes()
    return pl.pallas_call(
        functools.partial(_dilated_kernel, tq=tq, s=s, slopes=slopes),
        grid=(batch, nq),
        in_specs=[pl.BlockSpec((tq, A_OUT), lambda b, i: (b * nq + i, 0)),
                  pl.BlockSpec((s, A_OUT), lambda b, i: (b, 0)),
                  pl.BlockSpec((s, A_OUT), lambda b, i: (b, 0)),
                  pl.BlockSpec((tq, 1), lambda b, i: (b * nq + i, 0)),
                  pl.BlockSpec((1, 1, s), lambda b, i: (b, 0, 0)),
                  _const_spec(table.shape)],
        out_specs=pl.BlockSpec((tq, A_OUT), lambda b, i: (b * nq + i, 0)),
        out_shape=jax.ShapeDtypeStruct((t, A_OUT), BF16),
        scratch_shapes=[pltpu.VMEM((A_HEADS // 2, s, 2 * LANES), BF16),
                        pltpu.VMEM((SUBLANES, LANES), F32)],
        compiler_params=_params(2),
        name="dilated_attn",
    )(qa, ka, va, pos_col, pos_row, table)


MLA_HEADS_PER_STEP = 4


def _mla_kernel(q_ref, k_ref, v_ref, o_ref, vaug_ref, kmax_ref):
    lane = _lane_ids(q_ref.shape[0])

    @pl.when(pl.program_id(2) == 0)
    def _():
        for head in range(MLA_HEADS_PER_STEP):
            if head % 2 == 0:
                _fill_v_aug(vaug_ref.at[head // 2], v_ref[:, (head // 2) * LANES:(head // 2 + 1) * LANES])
            _store_max_key_norm(kmax_ref, head, k_ref[:, head * LANES:(head + 1) * LANES], 0, LANES)

    def run(exact):
        outs, sums = [], []
        for head in range(MLA_HEADS_PER_STEP):
            lo = head * LANES
            q = q_ref[:, lo:lo + LANES]
            sc = _dot_nt(q, k_ref[:, lo:lo + LANES])
            c = jnp.max(sc, axis=-1, keepdims=True) if exact else _score_bound(q, kmax_ref, head)
            acc, rowsum = _softmax_pv(sc, c, vaug_ref[head // 2])
            outs.append(acc / rowsum)
            sums.append(rowsum)
        return outs, sums

    def write(outs):
        for pair in range(MLA_HEADS_PER_STEP // 2):
            o_ref[:, pair * LANES:(pair + 1) * LANES] = jnp.where(
                lane < MLA_V, outs[2 * pair], outs[2 * pair + 1]).astype(BF16)

    _fast_then_exact(run, write)


def _mla_attention(qb, kb, vb, batch, s):
    t = qb.shape[0]
    tq = min(QUERY_TILE, s)
    nq = s // tq
    hps = MLA_HEADS_PER_STEP
    return pl.pallas_call(
        _mla_kernel,
        grid=(batch, B_HEADS // hps, nq),
        in_specs=[pl.BlockSpec((tq, hps * LANES), lambda b, h, i: (b * nq + i, h)),
                  pl.BlockSpec((s, hps * LANES), lambda b, h, i: (b, h)),
                  pl.BlockSpec((s, hps * MLA_V), lambda b, h, i: (b, h))],
        out_specs=pl.BlockSpec((tq, hps * MLA_V), lambda b, h, i: (b * nq + i, h)),
        out_shape=jax.ShapeDtypeStruct((t, B_OUT), BF16),
        scratch_shapes=[pltpu.VMEM((hps // 2, s, 2 * LANES), BF16),
                        pltpu.VMEM((SUBLANES, LANES), F32)],
        compiler_params=_params(3),
        name="mla_attn",
    )(qb, kb, vb)


def _diff_kernel(q_ref, k_ref, v_ref, pq_ref, pk_ref, lq1_ref, lk1_ref, lq2_ref, lk2_ref, gain_ref,
                 o_ref, vaug_ref, kmax_ref, *, tq, slopes, lam_init):
    lam = (jnp.exp(jnp.sum(lq1_ref[...] * lk1_ref[...], axis=-1, keepdims=True))
           - jnp.exp(jnp.sum(lq2_ref[...] * lk2_ref[...], axis=-1, keepdims=True)) + lam_init)
    lane = _lane_ids(tq)
    gain = gain_ref[...]
    windows = LANES // DIFF_QK

    @pl.when(pl.program_id(1) == 0)
    def _():
        for pair in range(C_HEADS // 2):
            _fill_v_aug(vaug_ref.at[pair], v_ref[:, pair * LANES:(pair + 1) * LANES])
            for w in range(windows):
                _store_max_key_norm(kmax_ref, windows * pair + w, k_ref[:, pair * LANES:(pair + 1) * LANES],
                                    w * DIFF_QK, DIFF_QK)

    dist = jnp.abs(pq_ref[...] - pk_ref[0]).astype(F32)

    def run(exact):
        outs, sums = [], []
        for head in range(C_HEADS):
            pair, i = divmod(head, 2)
            lo = pair * LANES
            q = q_ref[:, lo:lo + LANES]
            sdist = (slopes[head] * LOG2E) * dist
            maps = []
            for c_idx in range(2):
                w = 2 * i + c_idx
                qm = jnp.where(_window(lane, w * DIFF_QK, DIFF_QK), q, jnp.zeros_like(q))
                sc = _dot_nt(qm, k_ref[:, lo:lo + LANES]) - sdist
                if exact:
                    c = jnp.max(sc, axis=-1, keepdims=True)
                else:
                    c = _score_bound(qm, kmax_ref, windows * pair + w)
                acc, rowsum = _softmax_pv(sc, c, vaug_ref[pair])
                maps.append(acc / rowsum)
                sums.append(rowsum)
            outs.append(maps[0] - lam * maps[1])
        return outs, sums

    def write(outs):
        for pair in range(C_HEADS // 2):
            o = jnp.where(lane < DIFF_V, outs[2 * pair], outs[2 * pair + 1])
            sq = o * o
            ms0 = jnp.sum(jnp.where(lane < DIFF_V, sq, 0.0), axis=-1, keepdims=True) / DIFF_V
            ms1 = jnp.sum(jnp.where(lane < DIFF_V, 0.0, sq), axis=-1, keepdims=True) / DIFF_V
            ms = jnp.where(lane < DIFF_V, ms0, ms1)
            o_ref[:, pair * LANES:(pair + 1) * LANES] = (
                o * lax.rsqrt(ms + EPS) * gain * (1.0 - lam_init)).astype(BF16)

    _fast_then_exact(run, write)


def _diff_attention(qc, kc, vc, pos_col, pos_row, lq1, lk1, lq2, lk2, head_gain, layer_idx, batch, s):
    t = qc.shape[0]
    tq = min(QUERY_TILE, s)
    nq = s // tq
    _, slopes = _alibi_slopes()
    lam_init = 0.8 - 0.6 * math.exp(-0.3 * layer_idx)
    w = C_HEADS * 2 * DIFF_QK
    vec = lambda a: a.reshape(1, DIFF_QK)
    gain2 = jnp.concatenate([head_gain, head_gain]).reshape(1, LANES)
    return pl.pallas_call(
        functools.partial(_diff_kernel, tq=tq, slopes=slopes, lam_init=lam_init),
        grid=(batch, nq),
        in_specs=[pl.BlockSpec((tq, w), lambda b, i: (b * nq + i, 0)),
                  pl.BlockSpec((s, w), lambda b, i: (b, 0)),
                  pl.BlockSpec((s, C_OUT), lambda b, i: (b, 0)),
                  pl.BlockSpec((tq, 1), lambda b, i: (b * nq + i, 0)),
                  pl.BlockSpec((1, 1, s), lambda b, i: (b, 0, 0)),
                  _const_spec((1, DIFF_QK)), _const_spec((1, DIFF_QK)),
                  _const_spec((1, DIFF_QK)), _const_spec((1, DIFF_QK)),
                  _const_spec((1, LANES))],
        out_specs=pl.BlockSpec((tq, C_OUT), lambda b, i: (b * nq + i, 0)),
        out_shape=jax.ShapeDtypeStruct((t, C_OUT), BF16),
        scratch_shapes=[pltpu.VMEM((C_HEADS // 2, s, 2 * LANES), BF16),
                        pltpu.VMEM((SUBLANES, LANES), F32)],
        compiler_params=_params(2),
        name="diff_attn",
    )(qc, kc, vc, pos_col, pos_row, vec(lq1), vec(lk1), vec(lq2), vec(lk2), gain2)


def kernel(x, positions, ffn1_norm, ffn1_w_gate, ffn1_w_up, ffn1_w_down, mix_norm, w_in, mla_q_norm, mla_w_uq, mla_kv_norm, mla_w_ukv, diff_lambda_q1, diff_lambda_k1, diff_lambda_q2, diff_lambda_k2, diff_head_norm, w_out, ffn2_norm, ffn2_w_gate, ffn2_w_up, ffn2_w_down, final_norm):
    batch, s, d = x.shape
    t = batch * s
    depth = w_in.shape[0]
    h = x.reshape(t, d)
    pos_col = positions.reshape(t, 1)
    pos_row = positions.reshape(batch, 1, s)
    ctab, stab, root = _rotary_tables(positions)
    bf = lambda w: w.astype(BF16)

    for l in range(depth):
        h = _ffn(h, ffn1_norm[l], bf(ffn1_w_gate[l]), bf(ffn1_w_up[l]), bf(ffn1_w_down[l]))
        win, wuq, wukv = _rearranged_weights(w_in[l], mla_w_uq[l], mla_w_ukv[l])
        qa, ka, va, qb, kb, vb, qc, kc, vc = _project(
            h, mix_norm[l], win, mla_q_norm[l], wuq, mla_kv_norm[l], wukv, ctab, stab, root)
        out_a = _dilated_attention(qa, ka, va, pos_col, pos_row, batch, s)
        out_b = _mla_attention(qb, kb, vb, batch, s)
        out_c = _diff_attention(qc, kc, vc, pos_col, pos_row, diff_lambda_q1[l], diff_lambda_k1[l],
                                diff_lambda_q2[l], diff_lambda_k2[l], diff_head_norm[l], l, batch, s)
        wo = bf(w_out[l])
        mix = (out_a, out_b, out_c, wo[:A_OUT], wo[A_OUT:A_OUT + B_OUT], wo[A_OUT + B_OUT:])
        h = _ffn(h, ffn2_norm[l], bf(ffn2_w_gate[l]), bf(ffn2_w_up[l]), bf(ffn2_w_down[l]),
                 mix=mix, final_gain=final_norm if l == depth - 1 else None)
    return h.reshape(batch, s, d)
```

```python
import functools
import math

import numpy as np
import jax
import jax.numpy as jnp
from jax import lax
from jax.experimental import pallas as pl
from jax.experimental.pallas import tpu as pltpu

D_MODEL = 1024
HEAD_DIM = 64
A_HEADS = 4
DILATED_PATTERNS = ((128, 1), (512, 4), (2048, 16))
B_HEADS = 8
MLA_Q_LORA = 384
MLA_KV_LORA = 256
MLA_NOPE = 64
MLA_ROPE = 32
MLA_V = 64
ROPE_THETA = 10000.0
C_HEADS = 4
DIFF_QK = 32
DIFF_V = 64
D_FF = 2816
EPS = 1e-6
NEG_INF = -1e30

A_COLS = 3 * A_HEADS * HEAD_DIM
B_COLS = MLA_Q_LORA + MLA_KV_LORA + MLA_ROPE
C_COLS = 2 * (C_HEADS * 2 * DIFF_QK) + C_HEADS * DIFF_V
A_OUT = A_HEADS * HEAD_DIM
B_OUT = B_HEADS * MLA_V
C_OUT = C_HEADS * DIFF_V

LANES = 128
SUBLANES = 8
LOG2E = math.log2(math.e)
VMEM_LIMIT = 56 * 1024 * 1024

BF16 = jnp.bfloat16
F32 = jnp.float32

TOKEN_TILE = 512
QUERY_TILE = 512


def _alibi_slopes():
    n = A_HEADS + C_HEADS
    s = [2.0 ** (-8.0 * (i + 1) / n) for i in range(n)]
    return s[0::2], s[1::2]


def _rms(x, gain):
    return x * lax.rsqrt(jnp.mean(x * x, axis=-1, keepdims=True) + EPS) * gain


def _dot(a, b):
    return jnp.dot(a, b, preferred_element_type=F32)


def _dot_nt(a, b):
    return lax.dot_general(a, b, (((1,), (1,)), ((), ())), preferred_element_type=F32)


def _params(n_grid):
    return pltpu.CompilerParams(dimension_semantics=("arbitrary",) * n_grid,
                                vmem_limit_bytes=VMEM_LIMIT)


def _const_spec(shape):
    return pl.BlockSpec(shape, lambda *_: (0,) * len(shape), pipeline_mode=pl.Buffered(1))


def _layer_spec(stacked, layer):
    tail = stacked.shape[1:]
    return pl.BlockSpec((None,) + tail, lambda *_: (layer,) + (0,) * len(tail),
                        pipeline_mode=pl.Buffered(1))


def _trig_kernel(pos_ref, inv_ref, cos_ref, sin_ref):
    ang = pos_ref[...].astype(F32) * inv_ref[...]
    cos_ref[...] = jnp.cos(ang)
    sin_ref[...] = jnp.sin(ang)


def _rotary_tables(positions):
    half = MLA_ROPE // 2
    t = positions.size
    rep = LANES // half
    rows = t // rep
    pos_rep = jnp.repeat(positions.reshape(t), half).reshape(rows, LANES)
    inv = ROPE_THETA ** (-jnp.arange(half, dtype=F32) / half)
    inv_row = jnp.tile(inv, rep).reshape(1, LANES)
    tr = min(rows, 512)
    cos, sin = pl.pallas_call(
        _trig_kernel,
        grid=(rows // tr,),
        in_specs=[pl.BlockSpec((tr, LANES), lambda i: (i, 0)), _const_spec((1, LANES))],
        out_specs=[pl.BlockSpec((tr, LANES), lambda i: (i, 0))] * 2,
        out_shape=[jax.ShapeDtypeStruct((rows, LANES), F32)] * 2,
        compiler_params=_params(1),
        name="rotary_trig",
    )(pos_rep, inv_row)
    cos = cos.reshape(t, half)
    sin = sin.reshape(t, half)
    root = math.sqrt((MLA_NOPE + MLA_ROPE) ** -0.5 * LOG2E)
    ones = jnp.ones((t, MLA_NOPE), F32)
    zeros_n = jnp.zeros((t, MLA_NOPE), F32)
    zeros_p = jnp.zeros((t, LANES - MLA_NOPE - MLA_ROPE), F32)
    ctab = root * jnp.concatenate([ones, cos, cos, zeros_p], axis=1)
    stab = root * jnp.concatenate([zeros_n, -sin, sin, zeros_p], axis=1)
    return ctab, stab, root


def _ffn_kernel(*refs, with_mix, with_final):
    it = iter(refs)
    x_ref = next(it)
    if with_mix:
        oa_ref, ob_ref, oc_ref, wo_ref = (next(it) for _ in range(4))
    gain_ref, wg_ref, wu_ref, wd_ref = (next(it) for _ in range(4))
    if with_final:
        fgain_ref = next(it)
    o_ref = next(it)

    x = x_ref[...]
    if with_mix:
        x = (x + _dot(oa_ref[...], wo_ref[:A_OUT])
             + _dot(ob_ref[...], wo_ref[A_OUT:A_OUT + B_OUT])
             + _dot(oc_ref[...], wo_ref[A_OUT + B_OUT:]))
    xn = _rms(x, gain_ref[...]).astype(BF16)
    g = _dot(xn, wg_ref[...])
    u = _dot(xn, wu_ref[...])
    h = (g * jax.nn.sigmoid(g) * u).astype(BF16)
    y = x + 0.5 * _dot(h, wd_ref[...])
    if with_final:
        y = _rms(y, fgain_ref[...])
    o_ref[...] = y


def _ffn(x, layer, gain, wg, wu, wd, mix=None, final_gain=None):
    t, d = x.shape
    tm = min(TOKEN_TILE, t)
    row = lambda w: pl.BlockSpec((tm, w), lambda i: (i, 0))
    args, specs = [x], [row(d)]
    if mix is not None:
        oa, ob, oc, wo = mix
        args += [oa, ob, oc, wo]
        specs += [row(A_OUT), row(B_OUT), row(C_OUT), _layer_spec(wo, layer)]
    args += [gain, wg, wu, wd]
    specs += [_layer_spec(a, layer) for a in (gain, wg, wu, wd)]
    if final_gain is not None:
        args.append(final_gain.reshape(1, d))
        specs.append(_const_spec((1, d)))
    return pl.pallas_call(
        functools.partial(_ffn_kernel, with_mix=mix is not None, with_final=final_gain is not None),
        grid=(t // tm,),
        in_specs=specs,
        out_specs=row(d),
        out_shape=jax.ShapeDtypeStruct((t, d), F32),
        compiler_params=_params(1),
        name="ffn",
    )(*args)


_P_QA, _P_KA, _P_VA = 0, A_OUT, 2 * A_OUT
_P_CQ = A_COLS
_P_CKV = _P_CQ + MLA_Q_LORA
_P_KPE = _P_CKV + MLA_KV_LORA
_P_KPE_SW = _P_KPE + LANES
_P_QC = _P_KPE_SW + LANES
_P_KC = _P_QC + C_HEADS * 2 * DIFF_QK
_P_VC = _P_KC + C_HEADS * 2 * DIFF_QK
_P_END = _P_VC + C_OUT
_QB_W = B_HEADS * LANES


def _proj_kernel(x_ref, gmix_ref, win_ref, gq_ref, wuq_ref, gkv_ref, wukv_ref, ctab_ref, stab_ref,
                 qa_ref, ka_ref, va_ref, qb_ref, kb_ref, vb_ref, qc_ref, kc_ref, vc_ref, *, root):
    xn = _rms(x_ref[...], gmix_ref[...]).astype(BF16)
    p = _dot(xn, win_ref[...])
    qa_ref[...] = (p[:, _P_QA:_P_KA] * (HEAD_DIM ** -0.5 * LOG2E)).astype(BF16)
    ka_ref[...] = p[:, _P_KA:_P_VA].astype(BF16)
    va_ref[...] = p[:, _P_VA:_P_CQ].astype(BF16)
    qc_ref[...] = (p[:, _P_QC:_P_KC] * (DIFF_QK ** -0.5 * LOG2E)).astype(BF16)
    kc_ref[...] = p[:, _P_KC:_P_VC].astype(BF16)
    vc_ref[...] = p[:, _P_VC:_P_END].astype(BF16)

    ctab = ctab_ref[...]
    stab = stab_ref[...]
    cq = _rms(p[:, _P_CQ:_P_CKV], gq_ref[...]).astype(BF16)
    qq = _dot(cq, wuq_ref[...])
    for h in range(B_HEADS):
        lo = h * LANES
        qb_ref[:, lo:lo + LANES] = (qq[:, lo:lo + LANES] * ctab
                                    + qq[:, _QB_W + lo:_QB_W + lo + LANES] * stab).astype(BF16)
    ckv = _rms(p[:, _P_CKV:_P_KPE], gkv_ref[...]).astype(BF16)
    kk = _dot(ckv, wukv_ref[...])
    krot = p[:, _P_KPE:_P_KPE_SW] * ctab + p[:, _P_KPE_SW:_P_QC] * stab
    for h in range(B_HEADS):
        lo = h * LANES
        kb_ref[:, lo:lo + LANES] = (kk[:, lo:lo + LANES] * root + krot).astype(BF16)
    vb_ref[...] = kk[:, _QB_W:_QB_W + B_OUT].astype(BF16)


def _rearranged_weights(w_in, w_uq, w_ukv):
    half = MLA_ROPE // 2
    pad = LANES - MLA_NOPE - MLA_ROPE
    lead = w_in.shape[:-1]
    kpe = w_in[..., _P_KPE:_P_KPE + MLA_ROPE]
    z = lambda c: jnp.zeros(lead + (c,), w_in.dtype)
    win = jnp.concatenate([
        w_in[..., :_P_KPE],
        z(MLA_NOPE), kpe, z(pad),
        z(MLA_NOPE), kpe[..., half:], kpe[..., :half], z(pad),
        w_in[..., _P_KPE + MLA_ROPE:]], axis=-1)
    lead = w_uq.shape[:-1]
    uq = w_uq.reshape(lead + (B_HEADS, MLA_NOPE + MLA_ROPE))
    nope, rope = uq[..., :MLA_NOPE], uq[..., MLA_NOPE:]
    zp = jnp.zeros(lead + (B_HEADS, pad), w_uq.dtype)
    zn = jnp.zeros(lead + (B_HEADS, MLA_NOPE), w_uq.dtype)
    main = jnp.concatenate([nope, rope, zp], axis=-1).reshape(lead + (_QB_W,))
    swap = jnp.concatenate([zn, rope[..., half:], rope[..., :half], zp], axis=-1).reshape(lead + (_QB_W,))
    wuq = jnp.concatenate([main, swap], axis=-1)
    lead = w_ukv.shape[:-1]
    ukv = w_ukv.reshape(lead + (B_HEADS, MLA_NOPE + MLA_V))
    kn = jnp.concatenate([ukv[..., :MLA_NOPE], jnp.zeros(lead + (B_HEADS, LANES - MLA_NOPE), w_ukv.dtype)],
                         axis=-1).reshape(lead + (_QB_W,))
    wukv = jnp.concatenate([kn, ukv[..., MLA_NOPE:].reshape(lead + (B_OUT,))], axis=-1)
    return win.astype(BF16), wuq.astype(BF16), wukv.astype(BF16)


def _project(x, layer, gmix, win, gq, wuq, gkv, wukv, ctab, stab, root):
    t, d = x.shape
    tm = min(TOKEN_TILE, t)
    row = lambda w: pl.BlockSpec((tm, w), lambda i: (i, 0))
    widths = [A_OUT, A_OUT, A_OUT, _QB_W, _QB_W, B_OUT, C_HEADS * 2 * DIFF_QK, C_HEADS * 2 * DIFF_QK, C_OUT]
    params = (gmix, win, gq, wuq, gkv, wukv)
    return pl.pallas_call(
        functools.partial(_proj_kernel, root=root),
        grid=(t // tm,),
        in_specs=[row(d)] + [_layer_spec(a, layer) for a in params] + [row(LANES), row(LANES)],
        out_specs=[row(w) for w in widths],
        out_shape=[jax.ShapeDtypeStruct((t, w), BF16) for w in widths],
        compiler_params=_params(1),
        name="mixer_proj",
    )(x, *params, ctab, stab)


UNDERFLOW_GUARD = 2.0 ** -40


def _softmax_pv(s, c, v_aug):
    p = jnp.exp2(s - c).astype(BF16)
    o = _dot(p, v_aug)
    return o[:, :LANES], o[:, LANES:LANES + 1]


def _fill_v_aug(vaug_ref, v):
    vaug_ref[:, :LANES] = v
    vaug_ref[:, LANES:] = jnp.ones_like(v)


def _lane_ids(rows):
    return lax.broadcasted_iota(jnp.int32, (rows, LANES), 1)


def _window(lane, first, width):
    return (lane >= first) & (lane < first + width)


BOUND_SLACK = 1.01


def _store_max_key_norms(kmax_ref, row, k, width):
    kf = k.astype(F32)
    lane = lax.broadcasted_iota(jnp.int32, (LANES, LANES), 0)
    col = lax.broadcasted_iota(jnp.int32, (LANES, LANES), 1)
    member = jnp.where(_window(lane, col * width, width), 1.0, 0.0).astype(BF16)
    sums = _dot((kf * kf).astype(BF16), member)
    kmax_ref[row:row + 1, :] = BOUND_SLACK * jnp.max(sums, axis=0, keepdims=True)


def _score_bound(qm, kmax_ref, row, w):
    qf = qm.astype(F32)
    return jnp.sqrt(jnp.sum(qf * qf, axis=-1, keepdims=True) * kmax_ref[row:row + 1, w:w + 1])


def _fast_then_exact(run, write):
    outs, sums = run(False)
    write(outs)
    low = functools.reduce(jnp.minimum, sums)
    safe = jnp.min(low) >= UNDERFLOW_GUARD

    @pl.when(jnp.logical_not(safe))
    def _():
        write(run(True)[0])


def _dilated_log2_weight_table(tq, s):
    nq = s // tq
    cols = (2 * nq - 1) * tq
    delta = np.arange(cols)[None, :] - (nq - 1) * tq - np.arange(tq)[:, None]
    mult = np.zeros(delta.shape, np.int64)
    for window, dil in DILATED_PATTERNS:
        radius = window // 2 // dil
        mult += ((delta % dil == 0) & (np.abs(delta) <= radius * dil)).astype(np.int64)
    table = np.full(delta.shape, NEG_INF, np.float32)
    table[mult > 0] = np.log2(mult[mult > 0]).astype(np.float32)
    return jnp.asarray(table)


_MAX_LOG2_MULT = math.log2(len(DILATED_PATTERNS))


def _dilated_kernel(q_ref, k_ref, v_ref, pq_ref, pk_ref, tab_ref, o_ref, vaug_ref, kmax_ref, *, tq, s, slopes):
    qi = pl.program_id(1)
    nq = s // tq
    lane = _lane_ids(tq)

    @pl.when(qi == 0)
    def _():
        for pair in range(A_HEADS // 2):
            _fill_v_aug(vaug_ref.at[pair], v_ref[:, pair * LANES:(pair + 1) * LANES])
            _store_max_key_norms(kmax_ref, pair, k_ref[:, pair * LANES:(pair + 1) * LANES], HEAD_DIM)

    dist = jnp.abs(pq_ref[...] - pk_ref[0]).astype(F32)
    start = pl.multiple_of((nq - 1 - qi) * tq, LANES)

    def run(exact):
        outs, sums = [], []
        for head in range(A_HEADS):
            pair, i = divmod(head, 2)
            lo = pair * LANES
            q = q_ref[:, lo:lo + LANES]
            qm = jnp.where(_window(lane, i * HEAD_DIM, HEAD_DIM), q, jnp.zeros_like(q))
            sc = _dot_nt(qm, k_ref[:, lo:lo + LANES])
            sc = sc - (slopes[head] * LOG2E) * dist + tab_ref[:, pl.ds(start, s)]
            if exact:
                c = jnp.max(sc, axis=-1, keepdims=True)
            else:
                c = _score_bound(qm, kmax_ref, pair, i) + _MAX_LOG2_MULT
            acc, rowsum = _softmax_pv(sc, c, vaug_ref[pair])
            outs.append(acc / rowsum)
            sums.append(rowsum)
        return outs, sums

    def write(outs):
        for pair in range(A_HEADS // 2):
            o_ref[:, pair * LANES:(pair + 1) * LANES] = jnp.where(
                lane < HEAD_DIM, outs[2 * pair], outs[2 * pair + 1]).astype(BF16)

    _fast_then_exact(run, write)


def _dilated_attention(qa, ka, va, pos_col, pos_row, batch, s):
    t = qa.shape[0]
    tq = min(QUERY_TILE, s)
    nq = s // tq
    table = _dilated_log2_weight_table(tq, s)
    slopes, _ = _alibi_slopes()
    return pl.pallas_call(
        functools.partial(_dilated_kernel, tq=tq, s=s, slopes=slopes),
        grid=(batch, nq),
        in_specs=[pl.BlockSpec((tq, A_OUT), lambda b, i: (b * nq + i, 0)),
                  pl.BlockSpec((s, A_OUT), lambda b, i: (b, 0)),
                  pl.BlockSpec((s, A_OUT), lambda b, i: (b, 0)),
                  pl.BlockSpec((tq, 1), lambda b, i: (b * nq + i, 0)),
                  pl.BlockSpec((1, 1, s), lambda b, i: (b, 0, 0)),
                  _const_spec(table.shape)],
        out_specs=pl.BlockSpec((tq, A_OUT), lambda b, i: (b * nq + i, 0)),
        out_shape=jax.ShapeDtypeStruct((t, A_OUT), BF16),
        scratch_shapes=[pltpu.VMEM((A_HEADS // 2, s, 2 * LANES), BF16),
                        pltpu.VMEM((SUBLANES, LANES), F32)],
        compiler_params=_params(2),
        name="dilated_attn",
    )(qa, ka, va, pos_col, pos_row, table)


MLA_HEADS_PER_STEP = 4


def _mla_kernel(q_ref, k_ref, v_ref, o_ref, vaug_ref, kmax_ref):
    lane = _lane_ids(q_ref.shape[0])

    @pl.when(pl.program_id(2) == 0)
    def _():
        for head in range(MLA_HEADS_PER_STEP):
            if head % 2 == 0:
                _fill_v_aug(vaug_ref.at[head // 2], v_ref[:, (head // 2) * LANES:(head // 2 + 1) * LANES])
            _store_max_key_norms(kmax_ref, head, k_ref[:, head * LANES:(head + 1) * LANES], LANES)

    def run(exact):
        outs, sums = [], []
        for head in range(MLA_HEADS_PER_STEP):
            lo = head * LANES
            q = q_ref[:, lo:lo + LANES]
            sc = _dot_nt(q, k_ref[:, lo:lo + LANES])
            c = jnp.max(sc, axis=-1, keepdims=True) if exact else _score_bound(q, kmax_ref, head, 0)
            acc, rowsum = _softmax_pv(sc, c, vaug_ref[head // 2])
            outs.append(acc / rowsum)
            sums.append(rowsum)
        return outs, sums

    def write(outs):
        for pair in range(MLA_HEADS_PER_STEP // 2):
            o_ref[:, pair * LANES:(pair + 1) * LANES] = jnp.where(
                lane < MLA_V, outs[2 * pair], outs[2 * pair + 1]).astype(BF16)

    _fast_then_exact(run, write)


def _mla_attention(qb, kb, vb, batch, s):
    t = qb.shape[0]
    tq = min(QUERY_TILE, s)
    nq = s // tq
    hps = MLA_HEADS_PER_STEP
    return pl.pallas_call(
        _mla_kernel,
        grid=(batch, B_HEADS // hps, nq),
        in_specs=[pl.BlockSpec((tq, hps * LANES), lambda b, h, i: (b * nq + i, h)),
                  pl.BlockSpec((s, hps * LANES), lambda b, h, i: (b, h)),
                  pl.BlockSpec((s, hps * MLA_V), lambda b, h, i: (b, h))],
        out_specs=pl.BlockSpec((tq, hps * MLA_V), lambda b, h, i: (b * nq + i, h)),
        out_shape=jax.ShapeDtypeStruct((t, B_OUT), BF16),
        scratch_shapes=[pltpu.VMEM((hps // 2, s, 2 * LANES), BF16),
                        pltpu.VMEM((SUBLANES, LANES), F32)],
        compiler_params=_params(3),
        name="mla_attn",
    )(qb, kb, vb)


def _diff_kernel(q_ref, k_ref, v_ref, pq_ref, pk_ref, lq1_ref, lk1_ref, lq2_ref, lk2_ref, gain_ref,
                 o_ref, vaug_ref, kmax_ref, *, tq, slopes, lam_init):
    lam = (jnp.exp(jnp.sum(lq1_ref[...] * lk1_ref[...], axis=-1, keepdims=True))
           - jnp.exp(jnp.sum(lq2_ref[...] * lk2_ref[...], axis=-1, keepdims=True)) + lam_init)
    lane = _lane_ids(tq)
    gain = gain_ref[...]

    @pl.when(pl.program_id(1) == 0)
    def _():
        for pair in range(C_HEADS // 2):
            _fill_v_aug(vaug_ref.at[pair], v_ref[:, pair * LANES:(pair + 1) * LANES])
            _store_max_key_norms(kmax_ref, pair, k_ref[:, pair * LANES:(pair + 1) * LANES], DIFF_QK)

    dist = jnp.abs(pq_ref[...] - pk_ref[0]).astype(F32)

    def run(exact):
        outs, sums = [], []
        for head in range(C_HEADS):
            pair, i = divmod(head, 2)
            lo = pair * LANES
            q = q_ref[:, lo:lo + LANES]
            sdist = (slopes[head] * LOG2E) * dist
            maps = []
            for c_idx in range(2):
                w = 2 * i + c_idx
                qm = jnp.where(_window(lane, w * DIFF_QK, DIFF_QK), q, jnp.zeros_like(q))
                sc = _dot_nt(qm, k_ref[:, lo:lo + LANES]) - sdist
                if exact:
                    c = jnp.max(sc, axis=-1, keepdims=True)
                else:
                    c = _score_bound(qm, kmax_ref, pair, w)
                acc, rowsum = _softmax_pv(sc, c, vaug_ref[pair])
                maps.append(acc / rowsum)
                sums.append(rowsum)
            outs.append(maps[0] - lam * maps[1])
        return outs, sums

    def write(outs):
        for pair in range(C_HEADS // 2):
            o = jnp.where(lane < DIFF_V, outs[2 * pair], outs[2 * pair + 1])
            sq = o * o
            ms0 = jnp.sum(jnp.where(lane < DIFF_V, sq, 0.0), axis=-1, keepdims=True) / DIFF_V
            ms1 = jnp.sum(jnp.where(lane < DIFF_V, 0.0, sq), axis=-1, keepdims=True) / DIFF_V
            ms = jnp.where(lane < DIFF_V, ms0, ms1)
            o_ref[:, pair * LANES:(pair + 1) * LANES] = (
                o * lax.rsqrt(ms + EPS) * gain * (1.0 - lam_init)).astype(BF16)

    _fast_then_exact(run, write)


def _diff_attention(qc, kc, vc, pos_col, pos_row, lam_vecs, head_gain, layer_idx, batch, s):
    t = qc.shape[0]
    tq = min(QUERY_TILE, s)
    nq = s // tq
    _, slopes = _alibi_slopes()
    lam_init = 0.8 - 0.6 * math.exp(-0.3 * layer_idx)
    w = C_HEADS * 2 * DIFF_QK
    params = list(lam_vecs) + [head_gain]
    return pl.pallas_call(
        functools.partial(_diff_kernel, tq=tq, slopes=slopes, lam_init=lam_init),
        grid=(batch, nq),
        in_specs=[pl.BlockSpec((tq, w), lambda b, i: (b * nq + i, 0)),
                  pl.BlockSpec((s, w), lambda b, i: (b, 0)),
                  pl.BlockSpec((s, C_OUT), lambda b, i: (b, 0)),
                  pl.BlockSpec((tq, 1), lambda b, i: (b * nq + i, 0)),
                  pl.BlockSpec((1, 1, s), lambda b, i: (b, 0, 0))]
                 + [_layer_spec(a, layer_idx) for a in params],
        out_specs=pl.BlockSpec((tq, C_OUT), lambda b, i: (b * nq + i, 0)),
        out_shape=jax.ShapeDtypeStruct((t, C_OUT), BF16),
        scratch_shapes=[pltpu.VMEM((C_HEADS // 2, s, 2 * LANES), BF16),
                        pltpu.VMEM((SUBLANES, LANES), F32)],
        compiler_params=_params(2),
        name="diff_attn",
    )(qc, kc, vc, pos_col, pos_row, *params)


def kernel(x, positions, ffn1_norm, ffn1_w_gate, ffn1_w_up, ffn1_w_down, mix_norm, w_in, mla_q_norm, mla_w_uq, mla_kv_norm, mla_w_ukv, diff_lambda_q1, diff_lambda_k1, diff_lambda_q2, diff_lambda_k2, diff_head_norm, w_out, ffn2_norm, ffn2_w_gate, ffn2_w_up, ffn2_w_down, final_norm):
    batch, s, d = x.shape
    t = batch * s
    depth = w_in.shape[0]
    h = x.reshape(t, d)
    pos_col = positions.reshape(t, 1)
    pos_row = positions.reshape(batch, 1, s)
    ctab, stab, root = _rotary_tables(positions)
    bf = lambda w: w.astype(BF16)
    vec = lambda g: g.reshape(depth, 1, -1)

    ffn1 = (vec(ffn1_norm), bf(ffn1_w_gate), bf(ffn1_w_up), bf(ffn1_w_down))
    ffn2 = (vec(ffn2_norm), bf(ffn2_w_gate), bf(ffn2_w_up), bf(ffn2_w_down))
    win, wuq, wukv = _rearranged_weights(w_in, mla_w_uq, mla_w_ukv)
    wo = bf(w_out)
    lam_vecs = [vec(a) for a in (diff_lambda_q1, diff_lambda_k1, diff_lambda_q2, diff_lambda_k2)]
    head_gain = vec(jnp.concatenate([diff_head_norm, diff_head_norm], axis=-1))

    for l in range(depth):
        h = _ffn(h, l, *ffn1)
        qa, ka, va, qb, kb, vb, qc, kc, vc = _project(
            h, l, vec(mix_norm), win, vec(mla_q_norm), wuq, vec(mla_kv_norm), wukv, ctab, stab, root)
        out_a = _dilated_attention(qa, ka, va, pos_col, pos_row, batch, s)
        out_b = _mla_attention(qb, kb, vb, batch, s)
        out_c = _diff_attention(qc, kc, vc, pos_col, pos_row, lam_vecs, head_gain, l, batch, s)
        h = _ffn(h, l, *ffn2, mix=(out_a, out_b, out_c, wo),
                 final_gain=final_norm if l == depth - 1 else None)
    return h.reshape(batch, s, d)
```

```python
import functools
import math

import numpy as np
import jax
import jax.numpy as jnp
from jax import lax
from jax.experimental import pallas as pl
from jax.experimental.pallas import tpu as pltpu

D_MODEL = 1024
HEAD_DIM = 64
A_HEADS = 4
DILATED_PATTERNS = ((128, 1), (512, 4), (2048, 16))
B_HEADS = 8
MLA_Q_LORA = 384
MLA_KV_LORA = 256
MLA_NOPE = 64
MLA_ROPE = 32
MLA_V = 64
ROPE_THETA = 10000.0
C_HEADS = 4
DIFF_QK = 32
DIFF_V = 64
D_FF = 2816
EPS = 1e-6
NEG_INF = -1e30

A_COLS = 3 * A_HEADS * HEAD_DIM
B_COLS = MLA_Q_LORA + MLA_KV_LORA + MLA_ROPE
C_COLS = 2 * (C_HEADS * 2 * DIFF_QK) + C_HEADS * DIFF_V
A_OUT = A_HEADS * HEAD_DIM
B_OUT = B_HEADS * MLA_V
C_OUT = C_HEADS * DIFF_V

LANES = 128
SUBLANES = 8
LOG2E = math.log2(math.e)
VMEM_LIMIT = 56 * 1024 * 1024

BF16 = jnp.bfloat16
F32 = jnp.float32

TOKEN_TILE = 512
PROJ_SUBTILES = 2
QUERY_TILE = 512


def _alibi_slopes():
    n = A_HEADS + C_HEADS
    s = [2.0 ** (-8.0 * (i + 1) / n) for i in range(n)]
    return s[0::2], s[1::2]


def _rms(x, gain):
    return x * lax.rsqrt(jnp.mean(x * x, axis=-1, keepdims=True) + EPS) * gain


def _dot(a, b):
    return jnp.dot(a, b, preferred_element_type=F32)


def _params(n_grid):
    return pltpu.CompilerParams(dimension_semantics=("arbitrary",) * n_grid,
                                vmem_limit_bytes=VMEM_LIMIT)


def _const_spec(shape):
    return pl.BlockSpec(shape, lambda *_: (0,) * len(shape), pipeline_mode=pl.Buffered(1))


def _layer_spec(stacked, layer):
    tail = stacked.shape[1:]
    return pl.BlockSpec((None,) + tail, lambda *_: (layer,) + (0,) * len(tail),
                        pipeline_mode=pl.Buffered(1))


def _trig_kernel(pos_ref, inv_ref, cos_ref, sin_ref):
    ang = inv_ref[...] * pos_ref[...].astype(F32)
    cos_ref[...] = jnp.cos(ang)
    sin_ref[...] = jnp.sin(ang)


_ROPE_ROOT = math.sqrt((MLA_NOPE + MLA_ROPE) ** -0.5 * LOG2E)


def _rotary_tables(positions):
    half = MLA_ROPE // 2
    t = positions.size
    blk = min(t, 2048)
    inv = (ROPE_THETA ** (-jnp.arange(half, dtype=F32) / half)).reshape(half, 1)
    cos_t, sin_t = pl.pallas_call(
        _trig_kernel,
        grid=(t // blk,),
        in_specs=[pl.BlockSpec((1, blk), lambda i: (0, i)), _const_spec((half, 1))],
        out_specs=[pl.BlockSpec((half, blk), lambda i: (0, i))] * 2,
        out_shape=[jax.ShapeDtypeStruct((half, t), F32)] * 2,
        compiler_params=_params(1),
        name="rotary_trig",
    )(positions.reshape(1, t), inv)
    cs = jnp.concatenate([cos_t, sin_t], axis=0).T
    hi = cs.astype(BF16)
    lo = (cs - hi.astype(F32)).astype(BF16)
    spread = np.zeros((4 * half, 2 * LANES), np.float32)
    for j in range(half):
        for base in (0, 2 * half):
            spread[base + j, MLA_NOPE + j] = 1.0
            spread[base + j, MLA_NOPE + half + j] = 1.0
            spread[base + half + j, LANES + MLA_NOPE + j] = -1.0
            spread[base + half + j, LANES + MLA_NOPE + half + j] = 1.0
    return jnp.concatenate([hi, lo], axis=1), jnp.asarray(spread, BF16)


def _ffn_kernel(*refs, with_mix, with_final):
    it = iter(refs)
    x_ref = next(it)
    if with_mix:
        oa_ref, ob_ref, oc_ref, wo_ref = (next(it) for _ in range(4))
    gain_ref, wg_ref, wu_ref, wd_ref = (next(it) for _ in range(4))
    if with_final:
        fgain_ref = next(it)
    o_ref = next(it)

    x = x_ref[...]
    if with_mix:
        x = (x + _dot(oa_ref[...], wo_ref[:A_OUT])
             + _dot(ob_ref[...], wo_ref[A_OUT:A_OUT + B_OUT])
             + _dot(oc_ref[...], wo_ref[A_OUT + B_OUT:]))
    xn = _rms(x, gain_ref[...]).astype(BF16)
    g = _dot(xn, wg_ref[...])
    u = _dot(xn, wu_ref[...])
    h = (g * jax.nn.sigmoid(g) * u).astype(BF16)
    y = x + 0.5 * _dot(h, wd_ref[...])
    if with_final:
        y = _rms(y, fgain_ref[...])
    o_ref[...] = y


def _ffn(x, layer, gain, wg, wu, wd, mix=None, final_gain=None):
    t, d = x.shape
    tm = min(TOKEN_TILE, t)
    row = lambda w: pl.BlockSpec((tm, w), lambda i: (i, 0))
    args, specs = [x], [row(d)]
    if mix is not None:
        oa, ob, oc, wo = mix
        args += [oa, ob, oc, wo]
        specs += [row(A_OUT), row(B_OUT), row(C_OUT), _layer_spec(wo, layer)]
    args += [gain, wg, wu, wd]
    specs += [_layer_spec(a, layer) for a in (gain, wg, wu, wd)]
    if final_gain is not None:
        args.append(final_gain.reshape(1, d))
        specs.append(_const_spec((1, d)))
    return pl.pallas_call(
        functools.partial(_ffn_kernel, with_mix=mix is not None, with_final=final_gain is not None),
        grid=(t // tm,),
        in_specs=specs,
        out_specs=row(d),
        out_shape=jax.ShapeDtypeStruct((t, d), F32),
        compiler_params=_params(1),
        name="ffn",
    )(*args)


_P_QA, _P_KA, _P_VA = 0, A_OUT, 2 * A_OUT
_P_CQ = A_COLS
_P_CKV = _P_CQ + MLA_Q_LORA
_P_KPE = _P_CKV + MLA_KV_LORA
_P_QC = _P_KPE + LANES
_P_KC = _P_QC + C_HEADS * 2 * DIFF_QK
_P_VC = _P_KC + C_HEADS * 2 * DIFF_QK
_P_END = _P_VC + C_OUT
_QB_W = B_HEADS * LANES


def _proj_kernel(x_ref, gmix_ref, win_ref, gq_ref, wuq_ref, gkv_ref, wukv_ref, cs_ref, spread_ref,
                 qa_ref, kat_ref, va_ref, qb_ref, kbt_ref, vb_ref, qc_ref, kct_ref, vc_ref):
    half = MLA_ROPE // 2
    tm = x_ref.shape[0]
    sub = tm // PROJ_SUBTILES

    for part in range(PROJ_SUBTILES):
        rows = slice(part * sub, (part + 1) * sub)
        xn = _rms(x_ref[rows, :], gmix_ref[...]).astype(BF16)
        p = _dot(xn, win_ref[...])
        qa_ref[rows, :] = (p[:, _P_QA:_P_KA] * (HEAD_DIM ** -0.5 * LOG2E)).astype(BF16)
        kat_ref[:, rows] = p[:, _P_KA:_P_VA].T.astype(BF16)
        va_ref[rows, :] = p[:, _P_VA:_P_CQ].astype(BF16)
        qc_ref[rows, :] = (p[:, _P_QC:_P_KC] * (DIFF_QK ** -0.5 * LOG2E)).astype(BF16)
        kct_ref[:, rows] = p[:, _P_KC:_P_VC].T.astype(BF16)
        vc_ref[rows, :] = p[:, _P_VC:_P_END].astype(BF16)

        lane = lax.broadcasted_iota(jnp.int32, (sub, LANES), 1)
        tabs = _dot(cs_ref[rows, :], spread_ref[...])
        ctab = (tabs[:, :LANES] + jnp.where(lane < MLA_NOPE, 1.0, 0.0)) * _ROPE_ROOT
        stab = tabs[:, LANES:] * _ROPE_ROOT

        def rotary(blk):
            partner = jnp.where(lane < MLA_NOPE + half, pltpu.roll(blk, LANES - half, 1), pltpu.roll(blk, half, 1))
            return blk * ctab + partner * stab

        cq = _rms(p[:, _P_CQ:_P_CKV], gq_ref[...]).astype(BF16)
        qq = _dot(cq, wuq_ref[...])
        for h in range(B_HEADS):
            lo = h * LANES
            qb_ref[rows, lo:lo + LANES] = rotary(qq[:, lo:lo + LANES]).astype(BF16)
        ckv = _rms(p[:, _P_CKV:_P_KPE], gkv_ref[...]).astype(BF16)
        kk = _dot(ckv, wukv_ref[...])
        krot = rotary(p[:, _P_KPE:_P_QC])
        for h in range(B_HEADS):
            lo = h * LANES
            kbt_ref[lo:lo + LANES, rows] = (kk[:, lo:lo + LANES] * _ROPE_ROOT + krot).T.astype(BF16)
        vb_ref[rows, :] = kk[:, _QB_W:_QB_W + B_OUT].astype(BF16)


def _rearranged_weights(w_in, w_uq, w_ukv):
    half = MLA_ROPE // 2
    pad = LANES - MLA_NOPE - MLA_ROPE
    lead = w_in.shape[:-1]
    kpe = w_in[..., _P_KPE:_P_KPE + MLA_ROPE]
    z = lambda c: jnp.zeros(lead + (c,), w_in.dtype)
    win = jnp.concatenate([
        w_in[..., :_P_KPE],
        z(MLA_NOPE), kpe, z(pad),
        w_in[..., _P_KPE + MLA_ROPE:]], axis=-1)
    lead = w_uq.shape[:-1]
    uq = w_uq.reshape(lead + (B_HEADS, MLA_NOPE + MLA_ROPE))
    nope, rope = uq[..., :MLA_NOPE], uq[..., MLA_NOPE:]
    zp = jnp.zeros(lead + (B_HEADS, pad), w_uq.dtype)
    wuq = jnp.concatenate([nope, rope, zp], axis=-1).reshape(lead + (_QB_W,))
    lead = w_ukv.shape[:-1]
    ukv = w_ukv.reshape(lead + (B_HEADS, MLA_NOPE + MLA_V))
    kn = jnp.concatenate([ukv[..., :MLA_NOPE], jnp.zeros(lead + (B_HEADS, LANES - MLA_NOPE), w_ukv.dtype)],
                         axis=-1).reshape(lead + (_QB_W,))
    wukv = jnp.concatenate([kn, ukv[..., MLA_NOPE:].reshape(lead + (B_OUT,))], axis=-1)
    return win.astype(BF16), wuq.astype(BF16), wukv.astype(BF16)


def _project(x, layer, gmix, win, gq, wuq, gkv, wukv, cs, spread):
    t, d = x.shape
    tm = min(TOKEN_TILE, t)
    row = lambda w: pl.BlockSpec((tm, w), lambda i: (i, 0))
    col = lambda w: pl.BlockSpec((w, tm), lambda i: (0, i))
    w_c = C_HEADS * 2 * DIFF_QK
    outs = [(row, A_OUT), (col, A_OUT), (row, A_OUT), (row, _QB_W), (col, _QB_W), (row, B_OUT),
            (row, w_c), (col, w_c), (row, C_OUT)]
    params = (gmix, win, gq, wuq, gkv, wukv)
    return pl.pallas_call(
        _proj_kernel,
        grid=(t // tm,),
        in_specs=[row(d)] + [_layer_spec(a, layer) for a in params] + [row(cs.shape[1]), _const_spec(spread.shape)],
        out_specs=[spec(w) for spec, w in outs],
        out_shape=[jax.ShapeDtypeStruct((t, w) if spec is row else (w, t), BF16) for spec, w in outs],
        compiler_params=_params(1),
        name="mixer_proj",
    )(x, *params, cs, spread)


UNDERFLOW_GUARD = 2.0 ** -40


def _softmax_pv(s, c, v_aug):
    p = jnp.exp2(s - c).astype(BF16)
    o = _dot(p, v_aug)
    return o[:, :LANES], o[:, LANES:LANES + 1]


def _fill_v_aug(vaug_ref, v):
    vaug_ref[:, :LANES] = v
    vaug_ref[:, LANES:] = jnp.ones_like(v)


def _lane_ids(rows):
    return lax.broadcasted_iota(jnp.int32, (rows, LANES), 1)


def _window(lane, first, width):
    return (lane >= first) & (lane < first + width)


BOUND_SLACK = 1.01


def _store_max_key_norms(kmax_ref, blk, kt, width):
    kf = kt.astype(F32)
    win = lax.broadcasted_iota(jnp.int32, (SUBLANES, LANES), 0)
    feat = lax.broadcasted_iota(jnp.int32, (SUBLANES, LANES), 1)
    member = jnp.where(_window(feat, win * width, width), 1.0, 0.0).astype(BF16)
    sums = _dot(member, (kf * kf).astype(BF16))
    top = BOUND_SLACK * jnp.max(sums, axis=-1, keepdims=True)
    kmax_ref[blk] = jnp.broadcast_to(top, (SUBLANES, LANES))


def _score_bound(qm, kmax_ref, blk, w):
    qf = qm.astype(F32)
    return jnp.sqrt(jnp.sum(qf * qf, axis=-1, keepdims=True) * kmax_ref[blk, w:w + 1, 0:1])


def _fast_then_exact(run, write):
    outs, sums = run(False)
    write(outs)
    low = functools.reduce(jnp.minimum, sums)
    safe = jnp.min(low) >= UNDERFLOW_GUARD

    @pl.when(jnp.logical_not(safe))
    def _():
        write(run(True)[0])


def _dilated_log2_weight_table(tq, s):
    nq = s // tq
    cols = (2 * nq - 1) * tq
    delta = np.arange(cols)[None, :] - (nq - 1) * tq - np.arange(tq)[:, None]
    mult = np.zeros(delta.shape, np.int64)
    for window, dil in DILATED_PATTERNS:
        radius = window // 2 // dil
        mult += ((delta % dil == 0) & (np.abs(delta) <= radius * dil)).astype(np.int64)
    table = np.full(delta.shape, NEG_INF, np.float32)
    table[mult > 0] = np.log2(mult[mult > 0]).astype(np.float32)
    return jnp.asarray(table)


_MAX_LOG2_MULT = math.log2(len(DILATED_PATTERNS))


def _dilated_kernel(q_ref, kt_ref, v_ref, pq_ref, pk_ref, tab_ref, o_ref, vaug_ref, kmax_ref, *, tq, s, slopes):
    qi = pl.program_id(1)
    nq = s // tq
    lane = _lane_ids(tq)

    @pl.when(qi == 0)
    def _():
        for pair in range(A_HEADS // 2):
            _fill_v_aug(vaug_ref.at[pair], v_ref[:, pair * LANES:(pair + 1) * LANES])
            _store_max_key_norms(kmax_ref, pair, kt_ref[pair * LANES:(pair + 1) * LANES, :], HEAD_DIM)

    dist = jnp.abs(pq_ref[...] - pk_ref[0]).astype(F32)
    start = pl.multiple_of((nq - 1 - qi) * tq, LANES)

    def run(exact):
        outs, sums = [], []
        for head in range(A_HEADS):
            pair, i = divmod(head, 2)
            lo = pair * LANES
            q = q_ref[:, lo:lo + LANES]
            qm = jnp.where(_window(lane, i * HEAD_DIM, HEAD_DIM), q, jnp.zeros_like(q))
            sc = _dot(qm, kt_ref[lo:lo + LANES, :])
            sc = sc - (slopes[head] * LOG2E) * dist + tab_ref[:, pl.ds(start, s)]
            if exact:
                c = jnp.max(sc, axis=-1, keepdims=True)
            else:
                c = _score_bound(qm, kmax_ref, pair, i) + _MAX_LOG2_MULT
            acc, rowsum = _softmax_pv(sc, c, vaug_ref[pair])
            outs.append(acc / rowsum)
            sums.append(rowsum)
        return outs, sums

    def write(outs):
        for pair in range(A_HEADS // 2):
            o_ref[:, pair * LANES:(pair + 1) * LANES] = jnp.where(
                lane < HEAD_DIM, outs[2 * pair], outs[2 * pair + 1]).astype(BF16)

    _fast_then_exact(run, write)


def _dilated_attention(qa, kat, va, pos_col, pos_row, batch, s):
    t = qa.shape[0]
    tq = min(QUERY_TILE, s)
    nq = s // tq
    table = _dilated_log2_weight_table(tq, s)
    slopes, _ = _alibi_slopes()
    return pl.pallas_call(
        functools.partial(_dilated_kernel, tq=tq, s=s, slopes=slopes),
        grid=(batch, nq),
        in_specs=[pl.BlockSpec((tq, A_OUT), lambda b, i: (b * nq + i, 0)),
                  pl.BlockSpec((A_OUT, s), lambda b, i: (0, b)),
                  pl.BlockSpec((s, A_OUT), lambda b, i: (b, 0)),
                  pl.BlockSpec((tq, 1), lambda b, i: (b * nq + i, 0)),
                  pl.BlockSpec((1, 1, s), lambda b, i: (b, 0, 0)),
                  _const_spec(table.shape)],
        out_specs=pl.BlockSpec((tq, A_OUT), lambda b, i: (b * nq + i, 0)),
        out_shape=jax.ShapeDtypeStruct((t, A_OUT), BF16),
        scratch_shapes=[pltpu.VMEM((A_HEADS // 2, s, 2 * LANES), BF16),
                        pltpu.VMEM((A_HEADS // 2, SUBLANES, LANES), F32)],
        compiler_params=_params(2),
        name="dilated_attn",
    )(qa, kat, va, pos_col, pos_row, table)


MLA_HEADS_PER_STEP = 8


def _mla_kernel(q_ref, kt_ref, v_ref, o_ref, vaug_ref, kmax_ref):
    lane = _lane_ids(q_ref.shape[0])

    @pl.when(pl.program_id(2) == 0)
    def _():
        for head in range(MLA_HEADS_PER_STEP):
            if head % 2 == 0:
                _fill_v_aug(vaug_ref.at[head // 2], v_ref[:, (head // 2) * LANES:(head // 2 + 1) * LANES])
            _store_max_key_norms(kmax_ref, head, kt_ref[head * LANES:(head + 1) * LANES, :], LANES)

    def run(exact):
        outs, sums = [], []
        for head in range(MLA_HEADS_PER_STEP):
            lo = head * LANES
            q = q_ref[:, lo:lo + LANES]
            sc = _dot(q, kt_ref[lo:lo + LANES, :])
            c = jnp.max(sc, axis=-1, keepdims=True) if exact else _score_bound(q, kmax_ref, head, 0)
            acc, rowsum = _softmax_pv(sc, c, vaug_ref[head // 2])
            outs.append(acc / rowsum)
            sums.append(rowsum)
        return outs, sums

    def write(outs):
        for pair in range(MLA_HEADS_PER_STEP // 2):
            o_ref[:, pair * LANES:(pair + 1) * LANES] = jnp.where(
                lane < MLA_V, outs[2 * pair], outs[2 * pair + 1]).astype(BF16)

    _fast_then_exact(run, write)


def _mla_attention(qb, kbt, vb, batch, s):
    t = qb.shape[0]
    tq = min(QUERY_TILE, s)
    nq = s // tq
    hps = MLA_HEADS_PER_STEP
    return pl.pallas_call(
        _mla_kernel,
        grid=(batch, B_HEADS // hps, nq),
        in_specs=[pl.BlockSpec((tq, hps * LANES), lambda b, h, i: (b * nq + i, h)),
                  pl.BlockSpec((hps * LANES, s), lambda b, h, i: (h, b)),
                  pl.BlockSpec((s, hps * MLA_V), lambda b, h, i: (b, h))],
        out_specs=pl.BlockSpec((tq, hps * MLA_V), lambda b, h, i: (b * nq + i, h)),
        out_shape=jax.ShapeDtypeStruct((t, B_OUT), BF16),
        scratch_shapes=[pltpu.VMEM((hps // 2, s, 2 * LANES), BF16),
                        pltpu.VMEM((hps, SUBLANES, LANES), F32)],
        compiler_params=_params(3),
        name="mla_attn",
    )(qb, kbt, vb)


def _diff_kernel(q_ref, kt_ref, v_ref, pq_ref, pk_ref, lq1_ref, lk1_ref, lq2_ref, lk2_ref, gain_ref,
                 o_ref, vaug_ref, kmax_ref, *, tq, slopes, lam_init):
    lam = (jnp.exp(jnp.sum(lq1_ref[...] * lk1_ref[...], axis=-1, keepdims=True))
           - jnp.exp(jnp.sum(lq2_ref[...] * lk2_ref[...], axis=-1, keepdims=True)) + lam_init)
    lane = _lane_ids(tq)
    gain = gain_ref[...]

    @pl.when(pl.program_id(1) == 0)
    def _():
        for pair in range(C_HEADS // 2):
            _fill_v_aug(vaug_ref.at[pair], v_ref[:, pair * LANES:(pair + 1) * LANES])
            _store_max_key_norms(kmax_ref, pair, kt_ref[pair * LANES:(pair + 1) * LANES, :], DIFF_QK)

    dist = jnp.abs(pq_ref[...] - pk_ref[0]).astype(F32)

    def run(exact):
        outs, sums = [], []
        for head in range(C_HEADS):
            pair, i = divmod(head, 2)
            lo = pair * LANES
            q = q_ref[:, lo:lo + LANES]
            sdist = (slopes[head] * LOG2E) * dist
            maps = []
            for c_idx in range(2):
                w = 2 * i + c_idx
                qm = jnp.where(_window(lane, w * DIFF_QK, DIFF_QK), q, jnp.zeros_like(q))
                sc = _dot(qm, kt_ref[lo:lo + LANES, :]) - sdist
                if exact:
                    c = jnp.max(sc, axis=-1, keepdims=True)
                else:
                    c = _score_bound(qm, kmax_ref, pair, w)
                acc, rowsum = _softmax_pv(sc, c, vaug_ref[pair])
                maps.append(acc / rowsum)
                sums.append(rowsum)
            outs.append(maps[0] - lam * maps[1])
        return outs, sums

    def write(outs):
        for pair in range(C_HEADS // 2):
            o = jnp.where(lane < DIFF_V, outs[2 * pair], outs[2 * pair + 1])
            sq = o * o
            ms0 = jnp.sum(jnp.where(lane < DIFF_V, sq, 0.0), axis=-1, keepdims=True) / DIFF_V
            ms1 = jnp.sum(jnp.where(lane < DIFF_V, 0.0, sq), axis=-1, keepdims=True) / DIFF_V
            ms = jnp.where(lane < DIFF_V, ms0, ms1)
            o_ref[:, pair * LANES:(pair + 1) * LANES] = (
                o * lax.rsqrt(ms + EPS) * gain * (1.0 - lam_init)).astype(BF16)

    _fast_then_exact(run, write)


def _diff_attention(qc, kct, vc, pos_col, pos_row, lam_vecs, head_gain, layer_idx, batch, s):
    t = qc.shape[0]
    tq = min(QUERY_TILE, s)
    nq = s // tq
    _, slopes = _alibi_slopes()
    lam_init = 0.8 - 0.6 * math.exp(-0.3 * layer_idx)
    w = C_HEADS * 2 * DIFF_QK
    params = list(lam_vecs) + [head_gain]
    return pl.pallas_call(
        functools.partial(_diff_kernel, tq=tq, slopes=slopes, lam_init=lam_init),
        grid=(batch, nq),
        in_specs=[pl.BlockSpec((tq, w), lambda b, i: (b * nq + i, 0)),
                  pl.BlockSpec((w, s), lambda b, i: (0, b)),
                  pl.BlockSpec((s, C_OUT), lambda b, i: (b, 0)),
                  pl.BlockSpec((tq, 1), lambda b, i: (b * nq + i, 0)),
                  pl.BlockSpec((1, 1, s), lambda b, i: (b, 0, 0))]
                 + [_layer_spec(a, layer_idx) for a in params],
        out_specs=pl.BlockSpec((tq, C_OUT), lambda b, i: (b * nq + i, 0)),
        out_shape=jax.ShapeDtypeStruct((t, C_OUT), BF16),
        scratch_shapes=[pltpu.VMEM((C_HEADS // 2, s, 2 * LANES), BF16),
                        pltpu.VMEM((C_HEADS // 2, SUBLANES, LANES), F32)],
        compiler_params=_params(2),
        name="diff_attn",
    )(qc, kct, vc, pos_col, pos_row, *params)


def kernel(x, positions, ffn1_norm, ffn1_w_gate, ffn1_w_up, ffn1_w_down, mix_norm, w_in, mla_q_norm, mla_w_uq, mla_kv_norm, mla_w_ukv, diff_lambda_q1, diff_lambda_k1, diff_lambda_q2, diff_lambda_k2, diff_head_norm, w_out, ffn2_norm, ffn2_w_gate, ffn2_w_up, ffn2_w_down, final_norm):
    batch, s, d = x.shape
    t = batch * s
    depth = w_in.shape[0]
    h = x.reshape(t, d)
    pos_col = positions.reshape(t, 1)
    pos_row = positions.reshape(batch, 1, s)
    cs, spread = _rotary_tables(positions)
    bf = lambda w: w.astype(BF16)
    vec = lambda g: g.reshape(depth, 1, -1)

    ffn1 = (vec(ffn1_norm), bf(ffn1_w_gate), bf(ffn1_w_up), bf(ffn1_w_down))
    ffn2 = (vec(ffn2_norm), bf(ffn2_w_gate), bf(ffn2_w_up), bf(ffn2_w_down))
    win, wuq, wukv = _rearranged_weights(w_in, mla_w_uq, mla_w_ukv)
    wo = bf(w_out)
    lam_vecs = [vec(a) for a in (diff_lambda_q1, diff_lambda_k1, diff_lambda_q2, diff_lambda_k2)]
    head_gain = vec(jnp.concatenate([diff_head_norm, diff_head_norm], axis=-1))

    for l in range(depth):
        h = _ffn(h, l, *ffn1)
        qa, kat, va, qb, kbt, vb, qc, kct, vc = _project(
            h, l, vec(mix_norm), win, vec(mla_q_norm), wuq, vec(mla_kv_norm), wukv, cs, spread)
        out_a = _dilated_attention(qa, kat, va, pos_col, pos_row, batch, s)
        out_b = _mla_attention(qb, kbt, vb, batch, s)
        out_c = _diff_attention(qc, kct, vc, pos_col, pos_row, lam_vecs, head_gain, l, batch, s)
        h = _ffn(h, l, *ffn2, mix=(out_a, out_b, out_c, wo),
                 final_gain=final_norm if l == depth - 1 else None)
    return h.reshape(batch, s, d)
```

```python
import functools
import math

import numpy as np
import jax
import jax.numpy as jnp
from jax import lax
from jax.experimental import pallas as pl
from jax.experimental.pallas import tpu as pltpu

D_MODEL = 1024
HEAD_DIM = 64
A_HEADS = 4
DILATED_PATTERNS = ((128, 1), (512, 4), (2048, 16))
B_HEADS = 8
MLA_Q_LORA = 384
MLA_KV_LORA = 256
MLA_NOPE = 64
MLA_ROPE = 32
MLA_V = 64
ROPE_THETA = 10000.0
C_HEADS = 4
DIFF_QK = 32
DIFF_V = 64
D_FF = 2816
EPS = 1e-6
NEG_INF = -1e30

A_COLS = 3 * A_HEADS * HEAD_DIM
B_COLS = MLA_Q_LORA + MLA_KV_LORA + MLA_ROPE
C_COLS = 2 * (C_HEADS * 2 * DIFF_QK) + C_HEADS * DIFF_V
A_OUT = A_HEADS * HEAD_DIM
B_OUT = B_HEADS * MLA_V
C_OUT = C_HEADS * DIFF_V

LANES = 128
SUBLANES = 8
LOG2E = math.log2(math.e)
VMEM_LIMIT = 56 * 1024 * 1024

BF16 = jnp.bfloat16
F32 = jnp.float32

TOKEN_TILE = 512
PROJ_SUBTILES = 2
QUERY_TILE = 512


def _alibi_slopes():
    n = A_HEADS + C_HEADS
    s = [2.0 ** (-8.0 * (i + 1) / n) for i in range(n)]
    return s[0::2], s[1::2]


def _rms(x, gain):
    return x * lax.rsqrt(jnp.mean(x * x, axis=-1, keepdims=True) + EPS) * gain


def _dot(a, b):
    return jnp.dot(a, b, preferred_element_type=F32)


def _params(n_grid):
    return pltpu.CompilerParams(dimension_semantics=("arbitrary",) * n_grid,
                                vmem_limit_bytes=VMEM_LIMIT)


def _const_spec(shape):
    return pl.BlockSpec(shape, lambda *_: (0,) * len(shape), pipeline_mode=pl.Buffered(1))


def _layer_spec(stacked, layer):
    tail = stacked.shape[1:]
    return pl.BlockSpec((None,) + tail, lambda *_: (layer,) + (0,) * len(tail),
                        pipeline_mode=pl.Buffered(1))


def _trig_kernel(pos_ref, inv_ref, cos_ref, sin_ref):
    ang = inv_ref[...] * pos_ref[...].astype(F32)
    cos_ref[...] = jnp.cos(ang)
    sin_ref[...] = jnp.sin(ang)


_ROPE_ROOT = math.sqrt((MLA_NOPE + MLA_ROPE) ** -0.5 * LOG2E)


def _rotary_tables(positions):
    half = MLA_ROPE // 2
    t = positions.size
    blk = min(t, 2048)
    inv = (ROPE_THETA ** (-jnp.arange(half, dtype=F32) / half)).reshape(half, 1)
    cos_t, sin_t = pl.pallas_call(
        _trig_kernel,
        grid=(t // blk,),
        in_specs=[pl.BlockSpec((1, blk), lambda i: (0, i)), _const_spec((half, 1))],
        out_specs=[pl.BlockSpec((half, blk), lambda i: (0, i))] * 2,
        out_shape=[jax.ShapeDtypeStruct((half, t), F32)] * 2,
        compiler_params=_params(1),
        name="rotary_trig",
    )(positions.reshape(1, t), inv)
    cs = jnp.concatenate([cos_t, sin_t], axis=0).T
    hi = cs.astype(BF16)
    lo = (cs - hi.astype(F32)).astype(BF16)
    spread = np.zeros((4 * half, 2 * LANES), np.float32)
    for j in range(half):
        for base in (0, 2 * half):
            spread[base + j, MLA_NOPE + j] = 1.0
            spread[base + j, MLA_NOPE + half + j] = 1.0
            spread[base + half + j, LANES + MLA_NOPE + j] = -1.0
            spread[base + half + j, LANES + MLA_NOPE + half + j] = 1.0
    return jnp.concatenate([hi, lo], axis=1), jnp.asarray(spread, BF16)


def _ffn_kernel(*refs, with_mix, with_final):
    it = iter(refs)
    x_ref = next(it)
    if with_mix:
        oa_ref, ob_ref, oc_ref, wo_ref = (next(it) for _ in range(4))
    gain_ref, wg_ref, wu_ref, wd_ref = (next(it) for _ in range(4))
    if with_final:
        fgain_ref = next(it)
    o_ref = next(it)

    x = x_ref[...]
    if with_mix:
        x = (x + _dot(oa_ref[...], wo_ref[:A_OUT])
             + _dot(ob_ref[...], wo_ref[A_OUT:A_OUT + B_OUT])
             + _dot(oc_ref[...], wo_ref[A_OUT + B_OUT:]))
    xn = _rms(x, gain_ref[...]).astype(BF16)
    g = _dot(xn, wg_ref[...])
    u = _dot(xn, wu_ref[...])
    h = (g * jax.nn.sigmoid(g) * u).astype(BF16)
    y = x + 0.5 * _dot(h, wd_ref[...])
    if with_final:
        y = _rms(y, fgain_ref[...])
    o_ref[...] = y


def _ffn(x, layer, gain, wg, wu, wd, mix=None, final_gain=None):
    t, d = x.shape
    tm = min(TOKEN_TILE, t)
    row = lambda w: pl.BlockSpec((tm, w), lambda i: (i, 0))
    args, specs = [x], [row(d)]
    if mix is not None:
        oa, ob, oc, wo = mix
        args += [oa, ob, oc, wo]
        specs += [row(A_OUT), row(B_OUT), row(C_OUT), _layer_spec(wo, layer)]
    args += [gain, wg, wu, wd]
    specs += [_layer_spec(a, layer) for a in (gain, wg, wu, wd)]
    if final_gain is not None:
        args.append(final_gain.reshape(1, d))
        specs.append(_const_spec((1, d)))
    return pl.pallas_call(
        functools.partial(_ffn_kernel, with_mix=mix is not None, with_final=final_gain is not None),
        grid=(t // tm,),
        in_specs=specs,
        out_specs=row(d),
        out_shape=jax.ShapeDtypeStruct((t, d), F32),
        compiler_params=_params(1),
        name="ffn",
    )(*args)


_P_QA, _P_KA, _P_VA = 0, A_OUT, 2 * A_OUT
_P_CQ = A_COLS
_P_CKV = _P_CQ + MLA_Q_LORA
_P_KPE = _P_CKV + MLA_KV_LORA
_P_QC = _P_KPE + LANES
_P_KC = _P_QC + C_HEADS * 2 * DIFF_QK
_P_VC = _P_KC + C_HEADS * 2 * DIFF_QK
_P_END = _P_VC + C_OUT
_QB_W = B_HEADS * LANES


def _proj_kernel(x_ref, gmix_ref, win_ref, gq_ref, wuq_ref, gkv_ref, wukv_ref, cs_ref, spread_ref,
                 qa_ref, kat_ref, va_ref, qbt_ref, kb_ref, vbt_ref, qct_ref, kc_ref, vct_ref):
    half = MLA_ROPE // 2
    tm = x_ref.shape[0]
    sub = tm // PROJ_SUBTILES

    for part in range(PROJ_SUBTILES):
        rows = slice(part * sub, (part + 1) * sub)
        xn = _rms(x_ref[rows, :], gmix_ref[...]).astype(BF16)
        p = _dot(xn, win_ref[...])
        qa_ref[rows, :] = (p[:, _P_QA:_P_KA] * (HEAD_DIM ** -0.5 * LOG2E)).astype(BF16)
        kat_ref[:, rows] = p[:, _P_KA:_P_VA].T.astype(BF16)
        va_ref[rows, :] = p[:, _P_VA:_P_CQ].astype(BF16)
        qct_ref[:, rows] = (p[:, _P_QC:_P_KC] * (DIFF_QK ** -0.5 * LOG2E)).T.astype(BF16)
        kc_ref[rows, :] = p[:, _P_KC:_P_VC].astype(BF16)
        vct_ref[:, rows] = p[:, _P_VC:_P_END].T.astype(BF16)

        lane = lax.broadcasted_iota(jnp.int32, (sub, LANES), 1)
        tabs = _dot(cs_ref[rows, :], spread_ref[...])
        ctab = (tabs[:, :LANES] + jnp.where(lane < MLA_NOPE, 1.0, 0.0)) * _ROPE_ROOT
        stab = tabs[:, LANES:] * _ROPE_ROOT

        def rotary(blk):
            partner = jnp.where(lane < MLA_NOPE + half, pltpu.roll(blk, LANES - half, 1), pltpu.roll(blk, half, 1))
            return blk * ctab + partner * stab

        cq = _rms(p[:, _P_CQ:_P_CKV], gq_ref[...]).astype(BF16)
        qq = _dot(cq, wuq_ref[...])
        for h in range(B_HEADS):
            lo = h * LANES
            qbt_ref[lo:lo + LANES, rows] = rotary(qq[:, lo:lo + LANES]).T.astype(BF16)
        ckv = _rms(p[:, _P_CKV:_P_KPE], gkv_ref[...]).astype(BF16)
        kk = _dot(ckv, wukv_ref[...])
        krot = rotary(p[:, _P_KPE:_P_QC])
        for h in range(B_HEADS):
            lo = h * LANES
            kb_ref[rows, lo:lo + LANES] = (kk[:, lo:lo + LANES] * _ROPE_ROOT + krot).astype(BF16)
        vbt_ref[:, rows] = kk[:, _QB_W:_QB_W + B_OUT].T.astype(BF16)


def _rearranged_weights(w_in, w_uq, w_ukv):
    w_in, w_uq, w_ukv = w_in.astype(BF16), w_uq.astype(BF16), w_ukv.astype(BF16)
    pad = LANES - MLA_NOPE - MLA_ROPE
    lead = w_in.shape[:-1]
    kpe = w_in[..., _P_KPE:_P_KPE + MLA_ROPE]
    z = lambda c: jnp.zeros(lead + (c,), w_in.dtype)
    win = jnp.concatenate([
        w_in[..., :_P_KPE],
        z(MLA_NOPE), kpe, z(pad),
        w_in[..., _P_KPE + MLA_ROPE:]], axis=-1)
    lead = w_uq.shape[:-1]
    uq = w_uq.reshape(lead + (B_HEADS, MLA_NOPE + MLA_ROPE))
    nope, rope = uq[..., :MLA_NOPE], uq[..., MLA_NOPE:]
    zp = jnp.zeros(lead + (B_HEADS, pad), w_uq.dtype)
    wuq = jnp.concatenate([nope, rope, zp], axis=-1).reshape(lead + (_QB_W,))
    lead = w_ukv.shape[:-1]
    ukv = w_ukv.reshape(lead + (B_HEADS, MLA_NOPE + MLA_V))
    kn = jnp.concatenate([ukv[..., :MLA_NOPE], jnp.zeros(lead + (B_HEADS, LANES - MLA_NOPE), w_ukv.dtype)],
                         axis=-1).reshape(lead + (_QB_W,))
    wukv = jnp.concatenate([kn, ukv[..., MLA_NOPE:].reshape(lead + (B_OUT,))], axis=-1)
    return win, wuq, wukv


def _project(x, layer, gmix, win, gq, wuq, gkv, wukv, cs, spread):
    t, d = x.shape
    tm = min(TOKEN_TILE, t)
    row = lambda w: pl.BlockSpec((tm, w), lambda i: (i, 0))
    col = lambda w: pl.BlockSpec((w, tm), lambda i: (0, i))
    w_c = C_HEADS * 2 * DIFF_QK
    outs = [(row, A_OUT), (col, A_OUT), (row, A_OUT), (col, _QB_W), (row, _QB_W), (col, B_OUT),
            (col, w_c), (row, w_c), (col, C_OUT)]
    params = (gmix, win, gq, wuq, gkv, wukv)
    return pl.pallas_call(
        _proj_kernel,
        grid=(t // tm,),
        in_specs=[row(d)] + [_layer_spec(a, layer) for a in params] + [row(cs.shape[1]), _const_spec(spread.shape)],
        out_specs=[spec(w) for spec, w in outs],
        out_shape=[jax.ShapeDtypeStruct((t, w) if spec is row else (w, t), BF16) for spec, w in outs],
        compiler_params=_params(1),
        name="mixer_proj",
    )(x, *params, cs, spread)


UNDERFLOW_GUARD = 2.0 ** -40


def _probs(s, c):
    return jnp.exp2(s - c).astype(BF16)


def _softmax_pv(s, c, v_aug):
    o = _dot(_probs(s, c), v_aug)
    return o[:, :LANES], o[:, LANES:LANES + 1]


def _fill_v_aug(vaug_ref, v):
    vaug_ref[:, :LANES] = v
    vaug_ref[:, LANES:] = jnp.ones_like(v)


def _lane_ids(rows):
    return lax.broadcasted_iota(jnp.int32, (rows, LANES), 1)


def _window(lane, first, width):
    return (lane >= first) & (lane < first + width)


BOUND_SLACK = 1.01


def _store_max_key_norms(kmax_ref, blk, kt, width):
    kf = kt.astype(F32)
    win = lax.broadcasted_iota(jnp.int32, (SUBLANES, LANES), 0)
    feat = lax.broadcasted_iota(jnp.int32, (SUBLANES, LANES), 1)
    member = jnp.where(_window(feat, win * width, width), 1.0, 0.0).astype(BF16)
    sums = _dot(member, (kf * kf).astype(BF16))
    top = BOUND_SLACK * jnp.max(sums, axis=-1, keepdims=True)
    kmax_ref[blk] = jnp.broadcast_to(top, (SUBLANES, LANES))


def _store_max_key_norms_lanes(kmax_ref, blk, k, width):
    kf = k.astype(F32)
    feat = lax.broadcasted_iota(jnp.int32, (LANES, LANES), 0)
    win = lax.broadcasted_iota(jnp.int32, (LANES, LANES), 1)
    member = jnp.where(_window(feat, win * width, width), 1.0, 0.0).astype(BF16)
    sums = _dot((kf * kf).astype(BF16), member)
    top = BOUND_SLACK * jnp.max(sums, axis=0, keepdims=True)
    kmax_ref[blk] = jnp.broadcast_to(top, (SUBLANES, LANES))


def _score_bound(qm, kmax_ref, blk, w):
    qf = qm.astype(F32)
    return jnp.sqrt(jnp.sum(qf * qf, axis=-1, keepdims=True) * kmax_ref[blk, w:w + 1, 0:1])


def _fast_then_exact(run, write):
    outs, sums = run(False)
    write(outs)
    low = functools.reduce(jnp.minimum, sums)
    safe = jnp.min(low) >= UNDERFLOW_GUARD

    @pl.when(jnp.logical_not(safe))
    def _():
        write(run(True)[0])


V_ROWS_T = 80


def _fill_v_aug_t(vaug_ref, head, vt, width):
    row = lax.broadcasted_iota(jnp.int32, (V_ROWS_T - width, vt.shape[1]), 0)
    vaug_ref[head, :width, :] = vt
    vaug_ref[head, width:, :] = jnp.where(row == 0, 1.0, 0.0).astype(BF16)


def _row_ids(cols):
    return lax.broadcasted_iota(jnp.int32, (LANES, cols), 0)


def _score_bound_t(qmt, kmax_ref, blk, w):
    qf = qmt.astype(F32)
    return jnp.sqrt(jnp.sum(qf * qf, axis=0, keepdims=True) * kmax_ref[blk, 0:1, w:w + 1])


def _pv_t(vaug, p_t, width):
    ot = _dot(vaug, p_t)
    rowsum = ot[width:width + 1, :]
    return ot[:width, :] / rowsum, rowsum


def _dilated_log2_weight_table(tq, s):
    nq = s // tq
    cols = (2 * nq - 1) * tq
    delta = np.arange(cols)[None, :] - (nq - 1) * tq - np.arange(tq)[:, None]
    mult = np.zeros(delta.shape, np.int64)
    for window, dil in DILATED_PATTERNS:
        radius = window // 2 // dil
        mult += ((delta % dil == 0) & (np.abs(delta) <= radius * dil)).astype(np.int64)
    table = np.full(delta.shape, NEG_INF, np.float32)
    table[mult > 0] = np.log2(mult[mult > 0]).astype(np.float32)
    return jnp.asarray(table)


_MAX_LOG2_MULT = math.log2(len(DILATED_PATTERNS))


def _dilated_kernel(q_ref, kt_ref, v_ref, pq_ref, pk_ref, tab_ref, o_ref, vaug_ref, kmax_ref, *, tq, s, slopes):
    qi = pl.program_id(1)
    nq = s // tq
    lane = _lane_ids(tq)

    @pl.when(qi == 0)
    def _():
        for pair in range(A_HEADS // 2):
            _fill_v_aug(vaug_ref.at[pair], v_ref[:, pair * LANES:(pair + 1) * LANES])
            _store_max_key_norms(kmax_ref, pair, kt_ref[pair * LANES:(pair + 1) * LANES, :], HEAD_DIM)

    dist = jnp.abs(pq_ref[...] - pk_ref[0]).astype(F32)
    start = pl.multiple_of((nq - 1 - qi) * tq, LANES)

    def run(exact):
        outs, sums = [], []
        for head in range(A_HEADS):
            pair, i = divmod(head, 2)
            lo = pair * LANES
            q = q_ref[:, lo:lo + LANES]
            qm = jnp.where(_window(lane, i * HEAD_DIM, HEAD_DIM), q, jnp.zeros_like(q))
            sc = _dot(qm, kt_ref[lo:lo + LANES, :])
            sc = sc - (slopes[head] * LOG2E) * dist + tab_ref[:, pl.ds(start, s)]
            if exact:
                c = jnp.max(sc, axis=-1, keepdims=True)
            else:
                c = _score_bound(qm, kmax_ref, pair, i) + _MAX_LOG2_MULT
            acc, rowsum = _softmax_pv(sc, c, vaug_ref[pair])
            outs.append(acc / rowsum)
            sums.append(rowsum)
        return outs, sums

    def write(outs):
        for pair in range(A_HEADS // 2):
            o_ref[:, pair * LANES:(pair + 1) * LANES] = jnp.where(
                lane < HEAD_DIM, outs[2 * pair], outs[2 * pair + 1]).astype(BF16)

    _fast_then_exact(run, write)


def _dilated_attention(qa, kat, va, pos_col, pos_row, batch, s):
    t = qa.shape[0]
    tq = min(QUERY_TILE, s)
    nq = s // tq
    table = _dilated_log2_weight_table(tq, s)
    slopes, _ = _alibi_slopes()
    return pl.pallas_call(
        functools.partial(_dilated_kernel, tq=tq, s=s, slopes=slopes),
        grid=(batch, nq),
        in_specs=[pl.BlockSpec((tq, A_OUT), lambda b, i: (b * nq + i, 0)),
                  pl.BlockSpec((A_OUT, s), lambda b, i: (0, b)),
                  pl.BlockSpec((s, A_OUT), lambda b, i: (b, 0)),
                  pl.BlockSpec((tq, 1), lambda b, i: (b * nq + i, 0)),
                  pl.BlockSpec((1, 1, s), lambda b, i: (b, 0, 0)),
                  _const_spec(table.shape)],
        out_specs=pl.BlockSpec((tq, A_OUT), lambda b, i: (b * nq + i, 0)),
        out_shape=jax.ShapeDtypeStruct((t, A_OUT), BF16),
        scratch_shapes=[pltpu.VMEM((A_HEADS // 2, s, 2 * LANES), BF16),
                        pltpu.VMEM((A_HEADS // 2, SUBLANES, LANES), F32)],
        compiler_params=_params(2),
        name="dilated_attn",
    )(qa, kat, va, pos_col, pos_row, table)


MLA_HEADS_PER_STEP = 8


def _mla_kernel(qt_ref, k_ref, vt_ref, o_ref, vaug_ref, kmax_ref):
    tq = qt_ref.shape[1]

    @pl.when(pl.program_id(2) == 0)
    def _():
        for head in range(MLA_HEADS_PER_STEP):
            _fill_v_aug_t(vaug_ref, head, vt_ref[head * MLA_V:(head + 1) * MLA_V, :], MLA_V)
            _store_max_key_norms_lanes(kmax_ref, head, k_ref[:, head * LANES:(head + 1) * LANES], LANES)

    def run(exact):
        outs, sums = [], []
        for head in range(MLA_HEADS_PER_STEP):
            lo = head * LANES
            qt = qt_ref[lo:lo + LANES, :]
            st = _dot(k_ref[:, lo:lo + LANES], qt)
            c = jnp.max(st, axis=0, keepdims=True) if exact else _score_bound_t(qt, kmax_ref, head, 0)
            out, rowsum = _pv_t(vaug_ref[head], _probs(st, c), MLA_V)
            outs.append(out)
            sums.append(rowsum)
        return outs, sums

    def write(outs):
        for pair in range(MLA_HEADS_PER_STEP // 2):
            both = jnp.concatenate([outs[2 * pair], outs[2 * pair + 1]], axis=0)
            o_ref[:, pair * LANES:(pair + 1) * LANES] = both.T.astype(BF16)

    _fast_then_exact(run, write)


def _mla_attention(qbt, kb, vbt, batch, s):
    t = kb.shape[0]
    tq = min(QUERY_TILE, s)
    nq = s // tq
    hps = MLA_HEADS_PER_STEP
    return pl.pallas_call(
        _mla_kernel,
        grid=(batch, B_HEADS // hps, nq),
        in_specs=[pl.BlockSpec((hps * LANES, tq), lambda b, h, i: (h, b * nq + i)),
                  pl.BlockSpec((s, hps * LANES), lambda b, h, i: (b, h)),
                  pl.BlockSpec((hps * MLA_V, s), lambda b, h, i: (h, b))],
        out_specs=pl.BlockSpec((tq, hps * MLA_V), lambda b, h, i: (b * nq + i, h)),
        out_shape=jax.ShapeDtypeStruct((t, B_OUT), BF16),
        scratch_shapes=[pltpu.VMEM((hps, V_ROWS_T, s), BF16),
                        pltpu.VMEM((hps, SUBLANES, LANES), F32)],
        compiler_params=_params(3),
        name="mla_attn",
    )(qbt, kb, vbt)


def _diff_kernel(qt_ref, k_ref, vt_ref, pk_ref, pq_ref, lq1_ref, lk1_ref, lq2_ref, lk2_ref, gain_ref,
                 o_ref, vaug_ref, kmax_ref, *, tq, slopes, lam_init):
    lam = (jnp.exp(jnp.sum(lq1_ref[...] * lk1_ref[...], axis=-1, keepdims=True))
           - jnp.exp(jnp.sum(lq2_ref[...] * lk2_ref[...], axis=-1, keepdims=True)) + lam_init)
    row = _row_ids(tq)
    gain = gain_ref[...]
    windows = LANES // DIFF_QK

    @pl.when(pl.program_id(1) == 0)
    def _():
        for head in range(C_HEADS):
            _fill_v_aug_t(vaug_ref, head, vt_ref[head * DIFF_V:(head + 1) * DIFF_V, :], DIFF_V)
        for pair in range(C_HEADS // 2):
            _store_max_key_norms_lanes(kmax_ref, pair, k_ref[:, pair * LANES:(pair + 1) * LANES], DIFF_QK)

    dist = jnp.abs(pk_ref[...] - pq_ref[0]).astype(F32)

    def run(exact):
        outs, sums, maps = [], [], []
        for m in range(2 * C_HEADS):
            head = m // 2
            pair, w = divmod(m, windows)
            if m % 2 == 0:
                sdist = (slopes[head] * LOG2E) * dist
            qt = qt_ref[pair * LANES:(pair + 1) * LANES, :]
            qm = jnp.where(_window(row, w * DIFF_QK, DIFF_QK), qt, jnp.zeros_like(qt))
            st = _dot(k_ref[:, pair * LANES:(pair + 1) * LANES], qm) - sdist
            c = jnp.max(st, axis=0, keepdims=True) if exact else _score_bound_t(qm, kmax_ref, pair, w)
            out, rowsum = _pv_t(vaug_ref[head], _probs(st, c), DIFF_V)
            maps.append(out)
            sums.append(rowsum)
            if m % 2 == 1:
                o = maps[m - 1] - lam * maps[m]
                ms = jnp.mean(o * o, axis=0, keepdims=True)
                outs.append(o * lax.rsqrt(ms + EPS) * gain * (1.0 - lam_init))
        return outs, sums

    def write(outs):
        for pair in range(C_HEADS // 2):
            both = jnp.concatenate([outs[2 * pair], outs[2 * pair + 1]], axis=0)
            o_ref[:, pair * LANES:(pair + 1) * LANES] = both.T.astype(BF16)

    _fast_then_exact(run, write)


def _diff_attention(qct, kc, vct, pos_col, pos_row, lam_vecs, head_gain, layer_idx, batch, s):
    t = kc.shape[0]
    tq = min(QUERY_TILE, s)
    nq = s // tq
    _, slopes = _alibi_slopes()
    lam_init = 0.8 - 0.6 * math.exp(-0.3 * layer_idx)
    w = C_HEADS * 2 * DIFF_QK
    params = list(lam_vecs) + [head_gain]
    return pl.pallas_call(
        functools.partial(_diff_kernel, tq=tq, slopes=slopes, lam_init=lam_init),
        grid=(batch, nq),
        in_specs=[pl.BlockSpec((w, tq), lambda b, i: (0, b * nq + i)),
                  pl.BlockSpec((s, w), lambda b, i: (b, 0)),
                  pl.BlockSpec((C_OUT, s), lambda b, i: (0, b)),
                  pl.BlockSpec((s, 1), lambda b, i: (b, 0)),
                  pl.BlockSpec((1, 1, tq), lambda b, i: (b, 0, i))]
                 + [_layer_spec(a, layer_idx) for a in params],
        out_specs=pl.BlockSpec((tq, C_OUT), lambda b, i: (b * nq + i, 0)),
        out_shape=jax.ShapeDtypeStruct((t, C_OUT), BF16),
        scratch_shapes=[pltpu.VMEM((C_HEADS, V_ROWS_T, s), BF16),
                        pltpu.VMEM((C_HEADS // 2, SUBLANES, LANES), F32)],
        compiler_params=_params(2),
        name="diff_attn",
    )(qct, kc, vct, pos_col, pos_row, *params)


def kernel(x, positions, ffn1_norm, ffn1_w_gate, ffn1_w_up, ffn1_w_down, mix_norm, w_in, mla_q_norm, mla_w_uq, mla_kv_norm, mla_w_ukv, diff_lambda_q1, diff_lambda_k1, diff_lambda_q2, diff_lambda_k2, diff_head_norm, w_out, ffn2_norm, ffn2_w_gate, ffn2_w_up, ffn2_w_down, final_norm):
    batch, s, d = x.shape
    t = batch * s
    depth = w_in.shape[0]
    h = x.reshape(t, d)
    pos_col = positions.reshape(t, 1)
    pos_row = positions.reshape(batch, 1, s)
    cs, spread = _rotary_tables(positions)
    bf = lambda w: w.astype(BF16)
    vec = lambda g: g.reshape(depth, 1, -1)

    ffn1 = (vec(ffn1_norm), bf(ffn1_w_gate), bf(ffn1_w_up), bf(ffn1_w_down))
    ffn2 = (vec(ffn2_norm), bf(ffn2_w_gate), bf(ffn2_w_up), bf(ffn2_w_down))
    win, wuq, wukv = _rearranged_weights(w_in, mla_w_uq, mla_w_ukv)
    wo = bf(w_out)
    lam_vecs = [vec(a) for a in (diff_lambda_q1, diff_lambda_k1, diff_lambda_q2, diff_lambda_k2)]
    head_gain = diff_head_norm.reshape(depth, DIFF_V, 1)

    for l in range(depth):
        h = _ffn(h, l, *ffn1)
        qa, kat, va, qbt, kb, vbt, qct, kc, vct = _project(
            h, l, vec(mix_norm), win, vec(mla_q_norm), wuq, vec(mla_kv_norm), wukv, cs, spread)
        out_a = _dilated_attention(qa, kat, va, pos_col, pos_row, batch, s)
        out_b = _mla_attention(qbt, kb, vbt, batch, s)
        out_c = _diff_attention(qct, kc, vct, pos_col, pos_row, lam_vecs, head_gain, l, batch, s)
        h = _ffn(h, l, *ffn2, mix=(out_a, out_b, out_c, wo),
                 final_gain=final_norm if l == depth - 1 else None)
    return h.reshape(batch, s, d)
```

```python
import functools
import math

import numpy as np
import jax
import jax.numpy as jnp
from jax import lax
from jax.experimental import pallas as pl
from jax.experimental.pallas import tpu as pltpu

D_MODEL = 1024
HEAD_DIM = 64
A_HEADS = 4
DILATED_PATTERNS = ((128, 1), (512, 4), (2048, 16))
B_HEADS = 8
MLA_Q_LORA = 384
MLA_KV_LORA = 256
MLA_NOPE = 64
MLA_ROPE = 32
MLA_V = 64
ROPE_THETA = 10000.0
C_HEADS = 4
DIFF_QK = 32
DIFF_V = 64
D_FF = 2816
EPS = 1e-6
NEG_INF = -1e30

A_COLS = 3 * A_HEADS * HEAD_DIM
B_COLS = MLA_Q_LORA + MLA_KV_LORA + MLA_ROPE
C_COLS = 2 * (C_HEADS * 2 * DIFF_QK) + C_HEADS * DIFF_V
A_OUT = A_HEADS * HEAD_DIM
B_OUT = B_HEADS * MLA_V
C_OUT = C_HEADS * DIFF_V

LANES = 128
SUBLANES = 8
LOG2E = math.log2(math.e)
VMEM_LIMIT = 56 * 1024 * 1024

BF16 = jnp.bfloat16
F32 = jnp.float32

TOKEN_TILE = 512
PROJ_SUBTILES = 2
QUERY_TILE = 512


def _alibi_slopes():
    n = A_HEADS + C_HEADS
    s = [2.0 ** (-8.0 * (i + 1) / n) for i in range(n)]
    return s[0::2], s[1::2]


def _rms(x, gain):
    return x * lax.rsqrt(jnp.mean(x * x, axis=-1, keepdims=True) + EPS) * gain


def _dot(a, b):
    return jnp.dot(a, b, preferred_element_type=F32)


def _params(n_grid):
    return pltpu.CompilerParams(dimension_semantics=("arbitrary",) * n_grid,
                                vmem_limit_bytes=VMEM_LIMIT)


def _const_spec(shape):
    return pl.BlockSpec(shape, lambda *_: (0,) * len(shape), pipeline_mode=pl.Buffered(1))


def _layer_spec(stacked, layer):
    tail = stacked.shape[1:]
    return pl.BlockSpec((None,) + tail, lambda *_: (layer,) + (0,) * len(tail),
                        pipeline_mode=pl.Buffered(1))


def _trig_kernel(pos_ref, inv_ref, cos_ref, sin_ref):
    ang = inv_ref[...] * pos_ref[...].astype(F32)
    cos_ref[...] = jnp.cos(ang)
    sin_ref[...] = jnp.sin(ang)


_ROPE_ROOT = math.sqrt((MLA_NOPE + MLA_ROPE) ** -0.5 * LOG2E)


def _rotary_tables(positions):
    half = MLA_ROPE // 2
    t = positions.size
    blk = min(t, 2048)
    inv = (ROPE_THETA ** (-jnp.arange(half, dtype=F32) / half)).reshape(half, 1)
    cos_t, sin_t = pl.pallas_call(
        _trig_kernel,
        grid=(t // blk,),
        in_specs=[pl.BlockSpec((1, blk), lambda i: (0, i)), _const_spec((half, 1))],
        out_specs=[pl.BlockSpec((half, blk), lambda i: (0, i))] * 2,
        out_shape=[jax.ShapeDtypeStruct((half, t), F32)] * 2,
        compiler_params=_params(1),
        name="rotary_trig",
    )(positions.reshape(1, t), inv)
    cs = jnp.concatenate([cos_t, sin_t], axis=0).T
    hi = cs.astype(BF16)
    lo = (cs - hi.astype(F32)).astype(BF16)
    spread = np.zeros((4 * half, 2 * LANES), np.float32)
    for j in range(half):
        for base in (0, 2 * half):
            spread[base + j, MLA_NOPE + j] = 1.0
            spread[base + j, MLA_NOPE + half + j] = 1.0
            spread[base + half + j, LANES + MLA_NOPE + j] = -1.0
            spread[base + half + j, LANES + MLA_NOPE + half + j] = 1.0
    return jnp.concatenate([hi, lo], axis=1), jnp.asarray(spread, BF16)


def _ffn_kernel(*refs, with_mix, with_final):
    it = iter(refs)
    x_ref = next(it)
    if with_mix:
        oa_ref, ob_ref, oc_ref, wo_ref = (next(it) for _ in range(4))
    gain_ref, wg_ref, wu_ref, wd_ref = (next(it) for _ in range(4))
    if with_final:
        fgain_ref = next(it)
    o_ref = next(it)

    x = x_ref[...]
    if with_mix:
        x = (x + _dot(oa_ref[...], wo_ref[:A_OUT])
             + _dot(ob_ref[...], wo_ref[A_OUT:A_OUT + B_OUT])
             + _dot(oc_ref[...], wo_ref[A_OUT + B_OUT:]))
    xn = _rms(x, gain_ref[...]).astype(BF16)
    g = _dot(xn, wg_ref[...])
    u = _dot(xn, wu_ref[...])
    h = (g * jax.nn.sigmoid(g) * u).astype(BF16)
    y = x + 0.5 * _dot(h, wd_ref[...])
    if with_final:
        y = _rms(y, fgain_ref[...])
    o_ref[...] = y


def _ffn(x, layer, gain, wg, wu, wd, mix=None, final_gain=None):
    t, d = x.shape
    tm = min(TOKEN_TILE, t)
    row = lambda w: pl.BlockSpec((tm, w), lambda i: (i, 0))
    args, specs = [x], [row(d)]
    if mix is not None:
        oa, ob, oc, wo = mix
        args += [oa, ob, oc, wo]
        specs += [row(A_OUT), row(B_OUT), row(C_OUT), _layer_spec(wo, layer)]
    args += [gain, wg, wu, wd]
    specs += [_layer_spec(a, layer) for a in (gain, wg, wu, wd)]
    if final_gain is not None:
        args.append(final_gain.reshape(1, d))
        specs.append(_const_spec((1, d)))
    return pl.pallas_call(
        functools.partial(_ffn_kernel, with_mix=mix is not None, with_final=final_gain is not None),
        grid=(t // tm,),
        in_specs=specs,
        out_specs=row(d),
        out_shape=jax.ShapeDtypeStruct((t, d), F32),
        compiler_params=_params(1),
        name="ffn",
    )(*args)


_P_QA, _P_KA, _P_VA = 0, A_OUT, 2 * A_OUT
_P_CQ = A_COLS
_P_CKV = _P_CQ + MLA_Q_LORA
_P_KPE = _P_CKV + MLA_KV_LORA
_P_QC = _P_KPE + LANES
_P_KC = _P_QC + C_HEADS * 2 * DIFF_QK
_P_VC = _P_KC + C_HEADS * 2 * DIFF_QK
_P_END = _P_VC + C_OUT
_QB_W = B_HEADS * LANES


def _proj_kernel(x_ref, gmix_ref, win_ref, gq_ref, wuq_ref, gkv_ref, wukv_ref, cs_ref, spread_ref,
                 qa_ref, kat_ref, va_ref, qbt_ref, kb_ref, vbt_ref, qct_ref, kc_ref, vct_ref):
    half = MLA_ROPE // 2
    tm = x_ref.shape[0]
    sub = tm // PROJ_SUBTILES

    for part in range(PROJ_SUBTILES):
        rows = slice(part * sub, (part + 1) * sub)
        xn = _rms(x_ref[rows, :], gmix_ref[...]).astype(BF16)
        p = _dot(xn, win_ref[...])
        qa_ref[rows, :] = (p[:, _P_QA:_P_KA] * (HEAD_DIM ** -0.5 * LOG2E)).astype(BF16)
        kat_ref[:, rows] = p[:, _P_KA:_P_VA].T.astype(BF16)
        va_ref[rows, :] = p[:, _P_VA:_P_CQ].astype(BF16)
        qct_ref[:, rows] = (p[:, _P_QC:_P_KC] * (DIFF_QK ** -0.5 * LOG2E)).T.astype(BF16)
        kc_ref[rows, :] = p[:, _P_KC:_P_VC].astype(BF16)
        vct_ref[:, rows] = p[:, _P_VC:_P_END].T.astype(BF16)

        lane = lax.broadcasted_iota(jnp.int32, (sub, LANES), 1)
        tabs = _dot(cs_ref[rows, :], spread_ref[...])
        ctab = (tabs[:, :LANES] + jnp.where(lane < MLA_NOPE, 1.0, 0.0)) * _ROPE_ROOT
        stab = tabs[:, LANES:] * _ROPE_ROOT

        def rotary(blk):
            partner = jnp.where(lane < MLA_NOPE + half, pltpu.roll(blk, LANES - half, 1), pltpu.roll(blk, half, 1))
            return blk * ctab + partner * stab

        cq = _rms(p[:, _P_CQ:_P_CKV], gq_ref[...]).astype(BF16)
        qq = _dot(cq, wuq_ref[...])
        for h in range(B_HEADS):
            lo = h * LANES
            qbt_ref[lo:lo + LANES, rows] = rotary(qq[:, lo:lo + LANES]).T.astype(BF16)
        ckv = _rms(p[:, _P_CKV:_P_KPE], gkv_ref[...]).astype(BF16)
        kk = _dot(ckv, wukv_ref[...])
        krot = rotary(p[:, _P_KPE:_P_QC])
        for h in range(B_HEADS):
            lo = h * LANES
            kb_ref[rows, lo:lo + LANES] = (kk[:, lo:lo + LANES] * _ROPE_ROOT + krot).astype(BF16)
        vbt_ref[:, rows] = kk[:, _QB_W:_QB_W + B_OUT].T.astype(BF16)


def _rearranged_weights(w_in, w_uq, w_ukv):
    w_in, w_uq, w_ukv = w_in.astype(BF16), w_uq.astype(BF16), w_ukv.astype(BF16)
    pad = LANES - MLA_NOPE - MLA_ROPE
    lead = w_in.shape[:-1]
    kpe = w_in[..., _P_KPE:_P_KPE + MLA_ROPE]
    z = lambda c: jnp.zeros(lead + (c,), w_in.dtype)
    win = jnp.concatenate([
        w_in[..., :_P_KPE],
        z(MLA_NOPE), kpe, z(pad),
        w_in[..., _P_KPE + MLA_ROPE:]], axis=-1)
    lead = w_uq.shape[:-1]
    uq = w_uq.reshape(lead + (B_HEADS, MLA_NOPE + MLA_ROPE))
    nope, rope = uq[..., :MLA_NOPE], uq[..., MLA_NOPE:]
    zp = jnp.zeros(lead + (B_HEADS, pad), w_uq.dtype)
    wuq = jnp.concatenate([nope, rope, zp], axis=-1).reshape(lead + (_QB_W,))
    lead = w_ukv.shape[:-1]
    ukv = w_ukv.reshape(lead + (B_HEADS, MLA_NOPE + MLA_V))
    kn = jnp.concatenate([ukv[..., :MLA_NOPE], jnp.zeros(lead + (B_HEADS, LANES - MLA_NOPE), w_ukv.dtype)],
                         axis=-1).reshape(lead + (_QB_W,))
    wukv = jnp.concatenate([kn, ukv[..., MLA_NOPE:].reshape(lead + (B_OUT,))], axis=-1)
    return win, wuq, wukv


def _project(x, layer, gmix, win, gq, wuq, gkv, wukv, cs, spread):
    t, d = x.shape
    tm = min(TOKEN_TILE, t)
    row = lambda w: pl.BlockSpec((tm, w), lambda i: (i, 0))
    col = lambda w: pl.BlockSpec((w, tm), lambda i: (0, i))
    w_c = C_HEADS * 2 * DIFF_QK
    outs = [(row, A_OUT), (col, A_OUT), (row, A_OUT), (col, _QB_W), (row, _QB_W), (col, B_OUT),
            (col, w_c), (row, w_c), (col, C_OUT)]
    params = (gmix, win, gq, wuq, gkv, wukv)
    return pl.pallas_call(
        _proj_kernel,
        grid=(t // tm,),
        in_specs=[row(d)] + [_layer_spec(a, layer) for a in params] + [row(cs.shape[1]), _const_spec(spread.shape)],
        out_specs=[spec(w) for spec, w in outs],
        out_shape=[jax.ShapeDtypeStruct((t, w) if spec is row else (w, t), BF16) for spec, w in outs],
        compiler_params=_params(1),
        name="mixer_proj",
    )(x, *params, cs, spread)


UNDERFLOW_GUARD = 2.0 ** -40


def _probs(s, c):
    return jnp.exp2(s - c).astype(BF16)


def _softmax_pv(s, c, v_aug):
    o = _dot(_probs(s, c), v_aug)
    return o[:, :LANES], o[:, LANES:LANES + 1]


def _fill_v_aug(vaug_ref, v):
    vaug_ref[:, :LANES] = v
    vaug_ref[:, LANES:] = jnp.ones_like(v)


def _lane_ids(rows):
    return lax.broadcasted_iota(jnp.int32, (rows, LANES), 1)


def _window(lane, first, width):
    return (lane >= first) & (lane < first + width)


BOUND_SLACK = 1.01


def _store_max_key_norms(kmax_ref, blk, kt, width):
    kf = kt.astype(F32)
    win = lax.broadcasted_iota(jnp.int32, (SUBLANES, LANES), 0)
    feat = lax.broadcasted_iota(jnp.int32, (SUBLANES, LANES), 1)
    member = jnp.where(_window(feat, win * width, width), 1.0, 0.0).astype(BF16)
    sums = _dot(member, (kf * kf).astype(BF16))
    top = BOUND_SLACK * jnp.max(sums, axis=-1, keepdims=True)
    kmax_ref[blk] = jnp.broadcast_to(top, (SUBLANES, LANES))


def _store_max_key_norms_lanes(kmax_ref, blk, k, width):
    kf = k.astype(F32)
    feat = lax.broadcasted_iota(jnp.int32, (LANES, LANES), 0)
    win = lax.broadcasted_iota(jnp.int32, (LANES, LANES), 1)
    member = jnp.where(_window(feat, win * width, width), 1.0, 0.0).astype(BF16)
    sums = _dot((kf * kf).astype(BF16), member)
    top = BOUND_SLACK * jnp.max(sums, axis=0, keepdims=True)
    kmax_ref[blk] = jnp.broadcast_to(top, (SUBLANES, LANES))


def _score_bound(qm, kmax_ref, blk, w):
    qf = qm.astype(F32)
    return jnp.sqrt(jnp.sum(qf * qf, axis=-1, keepdims=True) * kmax_ref[blk, w:w + 1, 0:1])


def _fast_then_exact(run, write):
    outs, sums = run(False)
    write(outs)
    low = functools.reduce(jnp.minimum, sums)
    safe = jnp.min(low) >= UNDERFLOW_GUARD

    @pl.when(jnp.logical_not(safe))
    def _():
        write(run(True)[0])


V_ROWS_T = 80


def _fill_v_aug_t(vaug_ref, head, vt, width):
    row = lax.broadcasted_iota(jnp.int32, (V_ROWS_T - width, vt.shape[1]), 0)
    vaug_ref[head, :width, :] = vt
    vaug_ref[head, width:, :] = jnp.where(row == 0, 1.0, 0.0).astype(BF16)


def _row_ids(cols):
    return lax.broadcasted_iota(jnp.int32, (LANES, cols), 0)


def _score_bound_t(qmt, kmax_ref, blk, w):
    qf = qmt.astype(F32)
    return jnp.sqrt(jnp.sum(qf * qf, axis=0, keepdims=True) * kmax_ref[blk, 0:1, w:w + 1])


def _pv_t(vaug, p_t, width):
    ot = _dot(vaug, p_t)
    rowsum = ot[width:width + 1, :]
    return ot[:width, :] / rowsum, rowsum


def _dilated_log2_weight_table(tq, s):
    nq = s // tq
    cols = (2 * nq - 1) * tq
    delta = np.arange(cols)[None, :] - (nq - 1) * tq - np.arange(tq)[:, None]
    mult = np.zeros(delta.shape, np.int64)
    for window, dil in DILATED_PATTERNS:
        radius = window // 2 // dil
        mult += ((delta % dil == 0) & (np.abs(delta) <= radius * dil)).astype(np.int64)
    table = np.full(delta.shape, NEG_INF, np.float32)
    table[mult > 0] = np.log2(mult[mult > 0]).astype(np.float32)
    slabs = np.stack([table[:, (nq - 1 - qi) * tq:(nq - 1 - qi) * tq + s] for qi in range(nq)])
    return jnp.asarray(slabs)


_MAX_LOG2_MULT = math.log2(len(DILATED_PATTERNS))


def _dilated_kernel(q_ref, kt_ref, v_ref, pq_ref, pk_ref, tab_ref, o_ref, vaug_ref, kmax_ref, *, tq, s, slopes):
    qi = pl.program_id(1)
    nq = s // tq
    lane = _lane_ids(tq)

    @pl.when(qi == 0)
    def _():
        for pair in range(A_HEADS // 2):
            _fill_v_aug(vaug_ref.at[pair], v_ref[:, pair * LANES:(pair + 1) * LANES])
            _store_max_key_norms(kmax_ref, pair, kt_ref[pair * LANES:(pair + 1) * LANES, :], HEAD_DIM)

    dist = jnp.abs(pq_ref[...] - pk_ref[0]).astype(F32)

    def run(exact):
        outs, sums = [], []
        for head in range(A_HEADS):
            pair, i = divmod(head, 2)
            lo = pair * LANES
            q = q_ref[:, lo:lo + LANES]
            qm = jnp.where(_window(lane, i * HEAD_DIM, HEAD_DIM), q, jnp.zeros_like(q))
            sc = _dot(qm, kt_ref[lo:lo + LANES, :])
            sc = sc - (slopes[head] * LOG2E) * dist + tab_ref[...]
            if exact:
                c = jnp.max(sc, axis=-1, keepdims=True)
            else:
                c = _score_bound(qm, kmax_ref, pair, i) + _MAX_LOG2_MULT
            acc, rowsum = _softmax_pv(sc, c, vaug_ref[pair])
            outs.append(acc / rowsum)
            sums.append(rowsum)
        return outs, sums

    def write(outs):
        for pair in range(A_HEADS // 2):
            o_ref[:, pair * LANES:(pair + 1) * LANES] = jnp.where(
                lane < HEAD_DIM, outs[2 * pair], outs[2 * pair + 1]).astype(BF16)

    _fast_then_exact(run, write)


def _dilated_attention(qa, kat, va, pos_col, pos_row, batch, s):
    t = qa.shape[0]
    tq = min(QUERY_TILE, s)
    nq = s // tq
    table = _dilated_log2_weight_table(tq, s)
    slopes, _ = _alibi_slopes()
    return pl.pallas_call(
        functools.partial(_dilated_kernel, tq=tq, s=s, slopes=slopes),
        grid=(batch, nq),
        in_specs=[pl.BlockSpec((tq, A_OUT), lambda b, i: (b * nq + i, 0)),
                  pl.BlockSpec((A_OUT, s), lambda b, i: (0, b)),
                  pl.BlockSpec((s, A_OUT), lambda b, i: (b, 0)),
                  pl.BlockSpec((tq, 1), lambda b, i: (b * nq + i, 0)),
                  pl.BlockSpec((1, 1, s), lambda b, i: (b, 0, 0)),
                  pl.BlockSpec((None, tq, s), lambda b, i: (i, 0, 0))],
        out_specs=pl.BlockSpec((tq, A_OUT), lambda b, i: (b * nq + i, 0)),
        out_shape=jax.ShapeDtypeStruct((t, A_OUT), BF16),
        scratch_shapes=[pltpu.VMEM((A_HEADS // 2, s, 2 * LANES), BF16),
                        pltpu.VMEM((A_HEADS // 2, SUBLANES, LANES), F32)],
        compiler_params=_params(2),
        name="dilated_attn",
    )(qa, kat, va, pos_col, pos_row, table)


MLA_HEADS_PER_STEP = 8


def _mla_kernel(qt_ref, k_ref, vt_ref, o_ref, vaug_ref, kmax_ref):
    tq = qt_ref.shape[1]

    @pl.when(pl.program_id(2) == 0)
    def _():
        for head in range(MLA_HEADS_PER_STEP):
            _fill_v_aug_t(vaug_ref, head, vt_ref[head * MLA_V:(head + 1) * MLA_V, :], MLA_V)
            _store_max_key_norms_lanes(kmax_ref, head, k_ref[:, head * LANES:(head + 1) * LANES], LANES)

    def run(exact):
        outs, sums = [], []
        for head in range(MLA_HEADS_PER_STEP):
            lo = head * LANES
            qt = qt_ref[lo:lo + LANES, :]
            st = _dot(k_ref[:, lo:lo + LANES], qt)
            c = jnp.max(st, axis=0, keepdims=True) if exact else _score_bound_t(qt, kmax_ref, head, 0)
            out, rowsum = _pv_t(vaug_ref[head], _probs(st, c), MLA_V)
            outs.append(out)
            sums.append(rowsum)
        return outs, sums

    def write(outs):
        for pair in range(MLA_HEADS_PER_STEP // 2):
            both = jnp.concatenate([outs[2 * pair], outs[2 * pair + 1]], axis=0)
            o_ref[:, pair * LANES:(pair + 1) * LANES] = both.T.astype(BF16)

    _fast_then_exact(run, write)


def _mla_attention(qbt, kb, vbt, batch, s):
    t = kb.shape[0]
    tq = min(QUERY_TILE, s)
    nq = s // tq
    hps = MLA_HEADS_PER_STEP
    return pl.pallas_call(
        _mla_kernel,
        grid=(batch, B_HEADS // hps, nq),
        in_specs=[pl.BlockSpec((hps * LANES, tq), lambda b, h, i: (h, b * nq + i)),
                  pl.BlockSpec((s, hps * LANES), lambda b, h, i: (b, h)),
                  pl.BlockSpec((hps * MLA_V, s), lambda b, h, i: (h, b))],
        out_specs=pl.BlockSpec((tq, hps * MLA_V), lambda b, h, i: (b * nq + i, h)),
        out_shape=jax.ShapeDtypeStruct((t, B_OUT), BF16),
        scratch_shapes=[pltpu.VMEM((hps, V_ROWS_T, s), BF16),
                        pltpu.VMEM((hps, SUBLANES, LANES), F32)],
        compiler_params=_params(3),
        name="mla_attn",
    )(qbt, kb, vbt)


def _diff_kernel(qt_ref, k_ref, vt_ref, pk_ref, pq_ref, lq1_ref, lk1_ref, lq2_ref, lk2_ref, gain_ref,
                 o_ref, vaug_ref, kmax_ref, *, tq, slopes, lam_init):
    lam = (jnp.exp(jnp.sum(lq1_ref[...] * lk1_ref[...], axis=-1, keepdims=True))
           - jnp.exp(jnp.sum(lq2_ref[...] * lk2_ref[...], axis=-1, keepdims=True)) + lam_init)
    row = _row_ids(tq)
    gain = gain_ref[...]
    windows = LANES // DIFF_QK

    @pl.when(pl.program_id(1) == 0)
    def _():
        for head in range(C_HEADS):
            _fill_v_aug_t(vaug_ref, head, vt_ref[head * DIFF_V:(head + 1) * DIFF_V, :], DIFF_V)
        for pair in range(C_HEADS // 2):
            _store_max_key_norms_lanes(kmax_ref, pair, k_ref[:, pair * LANES:(pair + 1) * LANES], DIFF_QK)

    dist = jnp.abs(pk_ref[...] - pq_ref[0]).astype(F32)

    def run(exact):
        outs, sums, maps = [], [], []
        for m in range(2 * C_HEADS):
            head = m // 2
            pair, w = divmod(m, windows)
            if m % 2 == 0:
                sdist = (slopes[head] * LOG2E) * dist
            qt = qt_ref[pair * LANES:(pair + 1) * LANES, :]
            qm = jnp.where(_window(row, w * DIFF_QK, DIFF_QK), qt, jnp.zeros_like(qt))
            st = _dot(k_ref[:, pair * LANES:(pair + 1) * LANES], qm) - sdist
            c = jnp.max(st, axis=0, keepdims=True) if exact else _score_bound_t(qm, kmax_ref, pair, w)
            out, rowsum = _pv_t(vaug_ref[head], _probs(st, c), DIFF_V)
            maps.append(out)
            sums.append(rowsum)
            if m % 2 == 1:
                o = maps[m - 1] - lam * maps[m]
                ms = jnp.mean(o * o, axis=0, keepdims=True)
                outs.append(o * lax.rsqrt(ms + EPS) * gain * (1.0 - lam_init))
        return outs, sums

    def write(outs):
        for pair in range(C_HEADS // 2):
            both = jnp.concatenate([outs[2 * pair], outs[2 * pair + 1]], axis=0)
            o_ref[:, pair * LANES:(pair + 1) * LANES] = both.T.astype(BF16)

    _fast_then_exact(run, write)


def _diff_attention(qct, kc, vct, pos_col, pos_row, lam_vecs, head_gain, layer_idx, batch, s):
    t = kc.shape[0]
    tq = min(QUERY_TILE, s)
    nq = s // tq
    _, slopes = _alibi_slopes()
    lam_init = 0.8 - 0.6 * math.exp(-0.3 * layer_idx)
    w = C_HEADS * 2 * DIFF_QK
    params = list(lam_vecs) + [head_gain]
    return pl.pallas_call(
        functools.partial(_diff_kernel, tq=tq, slopes=slopes, lam_init=lam_init),
        grid=(batch, nq),
        in_specs=[pl.BlockSpec((w, tq), lambda b, i: (0, b * nq + i)),
                  pl.BlockSpec((s, w), lambda b, i: (b, 0)),
                  pl.BlockSpec((C_OUT, s), lambda b, i: (0, b)),
                  pl.BlockSpec((s, 1), lambda b, i: (b, 0)),
                  pl.BlockSpec((1, 1, tq), lambda b, i: (b, 0, i))]
                 + [_layer_spec(a, layer_idx) for a in params],
        out_specs=pl.BlockSpec((tq, C_OUT), lambda b, i: (b * nq + i, 0)),
        out_shape=jax.ShapeDtypeStruct((t, C_OUT), BF16),
        scratch_shapes=[pltpu.VMEM((C_HEADS, V_ROWS_T, s), BF16),
                        pltpu.VMEM((C_HEADS // 2, SUBLANES, LANES), F32)],
        compiler_params=_params(2),
        name="diff_attn",
    )(qct, kc, vct, pos_col, pos_row, *params)


def kernel(x, positions, ffn1_norm, ffn1_w_gate, ffn1_w_up, ffn1_w_down, mix_norm, w_in, mla_q_norm, mla_w_uq, mla_kv_norm, mla_w_ukv, diff_lambda_q1, diff_lambda_k1, diff_lambda_q2, diff_lambda_k2, diff_head_norm, w_out, ffn2_norm, ffn2_w_gate, ffn2_w_up, ffn2_w_down, final_norm):
    batch, s, d = x.shape
    t = batch * s
    depth = w_in.shape[0]
    h = x.reshape(t, d)
    pos_col = positions.reshape(t, 1)
    pos_row = positions.reshape(batch, 1, s)
    cs, spread = _rotary_tables(positions)
    bf = lambda w: w.astype(BF16)
    vec = lambda g: g.reshape(depth, 1, -1)

    ffn1 = (vec(ffn1_norm), bf(ffn1_w_gate), bf(ffn1_w_up), bf(ffn1_w_down))
    ffn2 = (vec(ffn2_norm), bf(ffn2_w_gate), bf(ffn2_w_up), bf(ffn2_w_down))
    win, wuq, wukv = _rearranged_weights(w_in, mla_w_uq, mla_w_ukv)
    wo = bf(w_out)
    lam_vecs = [vec(a) for a in (diff_lambda_q1, diff_lambda_k1, diff_lambda_q2, diff_lambda_k2)]
    head_gain = diff_head_norm.reshape(depth, DIFF_V, 1)

    for l in range(depth):
        h = _ffn(h, l, *ffn1)
        qa, kat, va, qbt, kb, vbt, qct, kc, vct = _project(
            h, l, vec(mix_norm), win, vec(mla_q_norm), wuq, vec(mla_kv_norm), wukv, cs, spread)
        out_a = _dilated_attention(qa, kat, va, pos_col, pos_row, batch, s)
        out_b = _mla_attention(qbt, kb, vbt, batch, s)
        out_c = _diff_attention(qct, kc, vct, pos_col, pos_row, lam_vecs, head_gain, l, batch, s)
        h = _ffn(h, l, *ffn2, mix=(out_a, out_b, out_c, wo),
                 final_gain=final_norm if l == depth - 1 else None)
    return h.reshape(batch, s, d)
```
